```python
import math, functools
import jax, jax.numpy as jnp
from jax import lax
import numpy as np


D_MODEL = 1024
BATCH = 16
SEQ = 2048
DEPTH = 2
DEC_BATCH = 32
DEC_SEQ = 8
PAST_LEN = 16384
PAGE_SIZE = 128

MIX_WIDTH = D_MODEL
SSM_WIDTH = MIX_WIDTH // 2
SSM_GROUP = 16
SSM_GROUPS = SSM_WIDTH // SSM_GROUP
SSM_STATE = 64
ATTN_WIDTH = MIX_WIDTH - SSM_WIDTH
HEAD_DIM = 64
N_HEADS = ATTN_WIDTH // HEAD_DIM
ROT_DIM = HEAD_DIM // 4
ROPE_THETA = 500000.0
MOBA_BLOCK = 256
MOBA_TOPK = 3
Q_BLOCK = 128
D_FF = ((8 * D_MODEL // 3 + 127) // 128) * 128
IN_WIDTH = SSM_WIDTH + 3 * ATTN_WIDTH
NORM_EPS = 1e-6
PAGES_PER_BLOCK = MOBA_BLOCK // PAGE_SIZE

kernel_name = 'hybrid_s5_moba_macaron_step'

F32 = jnp.float32


def rmsnorm(x, g):
    xf = x.astype(F32)
    y = xf * lax.rsqrt(jnp.mean(xf * xf, axis=-1, keepdims=True) + NORM_EPS) * g.astype(F32)
    return y.astype(x.dtype)


def swiglu(x, w_gate, w_up, w_down):
    return (jax.nn.silu(x @ w_gate) * (x @ w_up)) @ w_down


def rope(x, positions):
    half = ROT_DIM // 2
    inv_freq = jnp.power(ROPE_THETA, -jnp.arange(half, dtype=F32) * 2.0 / ROT_DIM)
    ang = positions.astype(F32)[:, None] * inv_freq[None, :]
    cos = jnp.cos(ang)[None, :, None, :]
    sin = jnp.sin(ang)[None, :, None, :]
    xf = x.astype(F32)
    x1 = xf[..., :half]
    x2 = xf[..., half:ROT_DIM]
    out = jnp.concatenate([x1 * cos - x2 * sin, x2 * cos + x1 * sin, xf[..., ROT_DIM:]], axis=-1)
    return out.astype(x.dtype)


def ssm_mixer(u, h0_re, h0_im, log_dt, a_re, a_im, b_re, b_im, c_re, c_im, d, glu_w, glu_b):
    bsz, seq_len, _ = u.shape
    uf = u.astype(F32).reshape(bsz, seq_len, SSM_GROUPS, SSM_GROUP)
    dt = jnp.exp(log_dt.astype(F32))[:, None]
    ar = a_re.astype(F32)
    ai = a_im.astype(F32)
    mag = jnp.exp(ar * dt)
    abar_re = mag * jnp.cos(ai * dt)
    abar_im = mag * jnp.sin(ai * dt)
    den = ar * ar + ai * ai
    n_re = abar_re - 1.0
    n_im = abar_im
    w_re = (n_re * ar + n_im * ai) / den
    w_im = (n_im * ar - n_re * ai) / den
    br = b_re.astype(F32)
    bi = b_im.astype(F32)
    bbar_re = w_re[..., None] * br - w_im[..., None] * bi
    bbar_im = w_re[..., None] * bi + w_im[..., None] * br
    bu_re = jnp.einsum('blgh,gph->blgp', uf, bbar_re)
    bu_im = jnp.einsum('blgh,gph->blgp', uf, bbar_im)
    h0r = h0_re.astype(F32)
    h0i = h0_im.astype(F32)
    bu_re = bu_re.at[:, 0].add(abar_re * h0r - abar_im * h0i)
    bu_im = bu_im.at[:, 0].add(abar_re * h0i + abar_im * h0r)
    a_r = jnp.broadcast_to(abar_re, bu_re.shape)
    a_i = jnp.broadcast_to(abar_im, bu_im.shape)

    def combine(e1, e2):
        a1r, a1i, b1r, b1i = e1
        a2r, a2i, b2r, b2i = e2
        return (a2r * a1r - a2i * a1i,
                a2r * a1i + a2i * a1r,
                a2r * b1r - a2i * b1i + b2r,
                a2r * b1i + a2i * b1r + b2i)

    _, _, hr, hi = lax.associative_scan(combine, (a_r, a_i, bu_re, bu_im), axis=1)
    y = (jnp.einsum('blgp,ghp->blgh', hr, c_re.astype(F32))
         - jnp.einsum('blgp,ghp->blgh', hi, c_im.astype(F32)))
    y = y.reshape(bsz, seq_len, SSM_WIDTH) + d.astype(F32) * u.astype(F32)
    z = jax.nn.gelu(y)
    out = z * jax.nn.sigmoid(z @ glu_w.astype(F32) + glu_b.astype(F32))
    return out.astype(u.dtype), hr[:, -1].astype(h0_re.dtype), hi[:, -1].astype(h0_im.dtype)


def select_blocks(q, qpos, kmean):
    gate = jnp.einsum('bhqd,bhnd->bhqn', q.astype(F32), kmean)
    nb = kmean.shape[2]
    past_ok = jnp.arange(nb)[None, :] < (qpos // MOBA_BLOCK)[:, None]
    gate = jnp.where(past_ok, gate, -jnp.inf)
    _, idx = lax.top_k(gate, MOBA_TOPK)
    valid = jnp.take_along_axis(jnp.broadcast_to(past_ok, gate.shape), idx, axis=-1)
    return idx, valid


def block_attend(q, k_sel, v_sel, sel_valid, k_own, v_own, own_mask):
    bsz, nh, nq, dh = q.shape
    k_sel = k_sel.reshape(bsz, nh, nq, MOBA_TOPK * MOBA_BLOCK, dh)
    v_sel = v_sel.reshape(bsz, nh, nq, MOBA_TOPK * MOBA_BLOCK, dh)
    sel_mask = jnp.repeat(sel_valid, MOBA_BLOCK, axis=-1)
    scale = HEAD_DIM ** -0.5
    s_sel = jnp.einsum('bhqd,bhqkd->bhqk', q, k_sel).astype(F32) * scale
    s_own = jnp.einsum('bhqd,bhod->bhqo', q, k_own).astype(F32) * scale
    s = jnp.concatenate([jnp.where(sel_mask, s_sel, -jnp.inf),
                         jnp.where(own_mask, s_own, -jnp.inf)], axis=-1)
    p = jax.nn.softmax(s, axis=-1).astype(v_own.dtype)
    n_sel = MOBA_TOPK * MOBA_BLOCK
    return (jnp.einsum('bhqk,bhqkd->bhqd', p[..., :n_sel], v_sel)
            + jnp.einsum('bhqo,bhod->bhqd', p[..., n_sel:], v_own))


def moba_prompt(q, k, v):
    bsz, seq_len, nh, dh = q.shape
    nb = max(-(-seq_len // MOBA_BLOCK), MOBA_TOPK)
    pad = nb * MOBA_BLOCK - seq_len

    def to_blocks(t):
        t = jnp.pad(t, ((0, 0), (0, pad), (0, 0), (0, 0)))
        return t.reshape(bsz, nb, MOBA_BLOCK, nh, dh).transpose(0, 3, 1, 2, 4)

    kb = to_blocks(k)
    vb = to_blocks(v)
    kmean = kb.astype(F32).mean(axis=3)
    qh = q.transpose(0, 2, 1, 3)
    b_idx = jnp.arange(bsz)[:, None, None, None]
    h_idx = jnp.arange(nh)[None, :, None, None]

    def chunk(c):
        start = c * Q_BLOCK
        qc = lax.dynamic_slice_in_dim(qh, start, Q_BLOCK, axis=2)
        qpos = start + jnp.arange(Q_BLOCK)
        idx, valid = select_blocks(qc, qpos, kmean)
        k_sel = kb[b_idx, h_idx, idx]
        v_sel = vb[b_idx, h_idx, idx]
        own = start // MOBA_BLOCK
        k_own = lax.dynamic_index_in_dim(kb, own, axis=2, keepdims=False)
        v_own = lax.dynamic_index_in_dim(vb, own, axis=2, keepdims=False)
        own_pos = own * MOBA_BLOCK + jnp.arange(MOBA_BLOCK)
        own_mask = own_pos[None, :] <= qpos[:, None]
        return block_attend(qc, k_sel, v_sel, valid, k_own, v_own, own_mask)

    out = lax.map(chunk, jnp.arange(seq_len // Q_BLOCK))
    return out.transpose(1, 0, 3, 2, 4).reshape(bsz, seq_len, nh * dh)


def moba_sample(q, k, v, k_pool, v_pool, page_table):
    bsz, n_new, nh, dh = q.shape
    n_pages = page_table.shape[1]
    past = n_pages * PAGE_SIZE
    nb_full = past // MOBA_BLOCK
    nb = max(nb_full, MOBA_TOPK)
    k_past = k_pool[page_table].reshape(bsz, past, nh, dh)
    kmean = (k_past[:, :nb_full * MOBA_BLOCK].astype(F32)
             .reshape(bsz, nb_full, MOBA_BLOCK, nh, dh).mean(axis=2).transpose(0, 2, 1, 3))
    kmean = jnp.pad(kmean, ((0, 0), (0, 0), (0, nb - nb_full), (0, 0)))
    qh = q.transpose(0, 2, 1, 3)
    qpos = past + jnp.arange(n_new)
    idx, valid = select_blocks(qh, qpos, kmean)
    pages = jnp.minimum(idx[..., None] * PAGES_PER_BLOCK + jnp.arange(PAGES_PER_BLOCK), n_pages - 1)
    phys = page_table[jnp.arange(bsz)[:, None, None, None, None], pages]
    h_idx = jnp.arange(nh)[None, :, None, None, None]
    k_sel = k_pool[phys, :, h_idx, :]
    v_sel = v_pool[phys, :, h_idx, :]
    tail_pages = min(PAGES_PER_BLOCK, n_pages)
    tail = tail_pages * PAGE_SIZE
    k_tail = k_past[:, past - tail:]
    v_tail = v_pool[page_table[:, n_pages - tail_pages:]].reshape(bsz, tail, nh, dh)
    k_own = jnp.concatenate([k_tail, k], axis=1).transpose(0, 2, 1, 3)
    v_own = jnp.concatenate([v_tail, v], axis=1).transpose(0, 2, 1, 3)
    own_pos = jnp.concatenate([past - tail + jnp.arange(tail), qpos])
    own_start = (qpos // MOBA_BLOCK) * MOBA_BLOCK
    own_mask = (own_pos[None, :] >= own_start[:, None]) & (own_pos[None, :] <= qpos[:, None])
    out = block_attend(qh, k_sel, v_sel, valid, k_own, v_own, own_mask)
    return out.transpose(0, 2, 1, 3).reshape(bsz, n_new, nh * dh)


def decoder_layer(x, positions, h0_re, h0_im, attend_fn, lp):
    x = x + 0.5 * swiglu(rmsnorm(x, lp['norm_ffn1']), lp['ffn1_w_gate'], lp['ffn1_w_up'], lp['ffn1_w_down'])
    h = rmsnorm(x, lp['norm_mix'])
    proj = h @ lp['w_in']
    bsz, seq_len, _ = x.shape
    u = proj[..., :SSM_WIDTH]
    q, k, v = jnp.split(proj[..., SSM_WIDTH:], 3, axis=-1)
    q = rope(q.reshape(bsz, seq_len, N_HEADS, HEAD_DIM), positions)
    k = rope(k.reshape(bsz, seq_len, N_HEADS, HEAD_DIM), positions)
    v = v.reshape(bsz, seq_len, N_HEADS, HEAD_DIM)
    y_ssm, h_re, h_im = ssm_mixer(u, h0_re, h0_im, lp['ssm_log_dt'], lp['ssm_a_re'], lp['ssm_a_im'],
                                  lp['ssm_b_re'], lp['ssm_b_im'], lp['ssm_c_re'], lp['ssm_c_im'],
                                  lp['ssm_d'], lp['glu_w'], lp['glu_b'])
    y_att = attend_fn(q, k, v)
    x = x + jnp.concatenate([y_ssm, y_att], axis=-1) @ lp['w_out']
    x = x + 0.5 * swiglu(rmsnorm(x, lp['norm_ffn2']), lp['ffn2_w_gate'], lp['ffn2_w_up'], lp['ffn2_w_down'])
    return x, k, v, h_re, h_im


def setup_inputs(seed: int = 0) -> dict:
    key = jax.random.key(seed)
    keys = jax.random.split(key, 32)

    def nrm(i, shape, scale):
        return scale * jax.random.normal(keys[i], shape, F32)

    n_pages = PAST_LEN // PAGE_SIZE
    n_used = DEC_BATCH * n_pages
    n_pool = n_used + n_used // 4
    perm = jax.random.permutation(keys[6], n_pool)
    page_table = perm[:n_used].reshape(DEC_BATCH, n_pages).astype(jnp.int32)
    mode = jnp.arange(SSM_STATE, dtype=F32)
    log_dt = jax.random.uniform(keys[11], (DEPTH, SSM_GROUPS), F32, math.log(1e-3), math.log(1e-1))
    return {
        'x_prompt': nrm(0, (BATCH, SEQ, D_MODEL), 1.0),
        'x_sample': nrm(1, (DEC_BATCH, DEC_SEQ, D_MODEL), 1.0),
        'cache_k': nrm(2, (DEPTH, n_pool, PAGE_SIZE, N_HEADS, HEAD_DIM), 1.0),
        'cache_v': nrm(3, (DEPTH, n_pool, PAGE_SIZE, N_HEADS, HEAD_DIM), 1.0),
        'state_ssm_re': nrm(4, (DEPTH, DEC_BATCH, SSM_GROUPS, SSM_STATE), 0.1),
        'state_ssm_im': nrm(5, (DEPTH, DEC_BATCH, SSM_GROUPS, SSM_STATE), 0.1),
        'page_table': page_table,
        'norm_ffn1': 1.0 + nrm(7, (DEPTH, D_MODEL), 0.02),
        'ffn1_w_gate': nrm(8, (DEPTH, D_MODEL, D_FF), D_MODEL ** -0.5),
        'ffn1_w_up': nrm(9, (DEPTH, D_MODEL, D_FF), D_MODEL ** -0.5),
        'ffn1_w_down': nrm(10, (DEPTH, D_FF, D_MODEL), D_FF ** -0.5),
        'norm_mix': 1.0 + nrm(12, (DEPTH, D_MODEL), 0.02),
        'w_in': nrm(13, (DEPTH, D_MODEL, IN_WIDTH), D_MODEL ** -0.5),
        'ssm_log_dt': log_dt,
        'ssm_a_re': -0.5 + nrm(14, (DEPTH, SSM_GROUPS, SSM_STATE), 0.01),
        'ssm_a_im': math.pi * mode + nrm(15, (DEPTH, SSM_GROUPS, SSM_STATE), 0.01),
        'ssm_b_re': nrm(16, (DEPTH, SSM_GROUPS, SSM_STATE, SSM_GROUP), (2 * SSM_GROUP) ** -0.5),
        'ssm_b_im': nrm(17, (DEPTH, SSM_GROUPS, SSM_STATE, SSM_GROUP), (2 * SSM_GROUP) ** -0.5),
        'ssm_c_re': nrm(18, (DEPTH, SSM_GROUPS, SSM_GROUP, SSM_STATE), SSM_STATE ** -0.5),
        'ssm_c_im': nrm(19, (DEPTH, SSM_GROUPS, SSM_GROUP, SSM_STATE), SSM_STATE ** -0.5),
        'ssm_d': nrm(20, (DEPTH, SSM_WIDTH), 1.0),
        'glu_w': nrm(21, (DEPTH, SSM_WIDTH, SSM_WIDTH), SSM_WIDTH ** -0.5),
        'glu_b': nrm(22, (DEPTH, SSM_WIDTH), 0.01),
        'w_out': nrm(23, (DEPTH, MIX_WIDTH, D_MODEL), MIX_WIDTH ** -0.5),
        'norm_ffn2': 1.0 + nrm(24, (DEPTH, D_MODEL), 0.02),
        'ffn2_w_gate': nrm(25, (DEPTH, D_MODEL, D_FF), D_MODEL ** -0.5),
        'ffn2_w_up': nrm(26, (DEPTH, D_MODEL, D_FF), D_MODEL ** -0.5),
        'ffn2_w_down': nrm(27, (DEPTH, D_FF, D_MODEL), D_FF ** -0.5),
        'norm_final': 1.0 + nrm(28, (D_MODEL,), 0.02),
    }


def reference(x_prompt, x_sample, cache_k, cache_v, state_ssm_re, state_ssm_im, page_table,
              norm_ffn1, ffn1_w_gate, ffn1_w_up, ffn1_w_down, norm_mix, w_in,
              ssm_log_dt, ssm_a_re, ssm_a_im, ssm_b_re, ssm_b_im, ssm_c_re, ssm_c_im, ssm_d,
              glu_w, glu_b, w_out, norm_ffn2, ffn2_w_gate, ffn2_w_up, ffn2_w_down, norm_final):
    bsz_p, seq_p, _ = x_prompt.shape
    bsz_s, seq_s, _ = x_sample.shape
    past = page_table.shape[1] * PAGE_SIZE
    pos_p = jnp.arange(seq_p)
    pos_s = past + jnp.arange(seq_s)
    h0_p = jnp.zeros((bsz_p, SSM_GROUPS, SSM_STATE), x_prompt.dtype)
    xp = x_prompt
    xs = x_sample
    k_p, v_p, hr_p, hi_p = [], [], [], []
    k_s, v_s, hr_s, hi_s = [], [], [], []
    for l in range(DEPTH):
        lp = {
            'norm_ffn1': norm_ffn1[l], 'ffn1_w_gate': ffn1_w_gate[l], 'ffn1_w_up': ffn1_w_up[l],
            'ffn1_w_down': ffn1_w_down[l], 'norm_mix': norm_mix[l], 'w_in': w_in[l],
            'ssm_log_dt': ssm_log_dt[l], 'ssm_a_re': ssm_a_re[l], 'ssm_a_im': ssm_a_im[l],
            'ssm_b_re': ssm_b_re[l], 'ssm_b_im': ssm_b_im[l], 'ssm_c_re': ssm_c_re[l],
            'ssm_c_im': ssm_c_im[l], 'ssm_d': ssm_d[l], 'glu_w': glu_w[l], 'glu_b': glu_b[l],
            'w_out': w_out[l], 'norm_ffn2': norm_ffn2[l], 'ffn2_w_gate': ffn2_w_gate[l],
            'ffn2_w_up': ffn2_w_up[l], 'ffn2_w_down': ffn2_w_down[l],
        }
        xp, kl, vl, hrl, hil = decoder_layer(xp, pos_p, h0_p, h0_p, moba_prompt, lp)
        k_p.append(kl); v_p.append(vl); hr_p.append(hrl); hi_p.append(hil)
        sample_attn = functools.partial(moba_sample, k_pool=cache_k[l], v_pool=cache_v[l],
                                        page_table=page_table)
        xs, kl, vl, hrl, hil = decoder_layer(xs, pos_s, state_ssm_re[l], state_ssm_im[l], sample_attn, lp)
        k_s.append(kl); v_s.append(vl); hr_s.append(hrl); hi_s.append(hil)
    y_prompt = rmsnorm(xp, norm_final)
    y_sample = rmsnorm(xs, norm_final)
    return (y_prompt, y_sample,
            jnp.stack(k_p), jnp.stack(v_p), jnp.stack(hr_p), jnp.stack(hi_p),
            jnp.stack(k_s), jnp.stack(v_s), jnp.stack(hr_s), jnp.stack(hi_s))
```

```python
import functools
import math

import numpy as np
import jax
import jax.numpy as jnp
from jax import lax
from jax.experimental import pallas as pl
from jax.experimental.pallas import tpu as pltpu

F32 = jnp.float32
BF16 = jnp.bfloat16

D_MODEL = 1024
SSM_WIDTH = 512
SSM_GROUP = 16
SSM_GROUPS = 32
SSM_STATE = 64
ATTN_WIDTH = 512
HEAD_DIM = 64
N_HEADS = 8
ROT_DIM = 16
ROPE_THETA = 500000.0
MOBA_BLOCK = 256
MOBA_TOPK = 3
PAGE_SIZE = 128
PAGES_PER_BLOCK = MOBA_BLOCK // PAGE_SIZE
D_FF = 2816
NORM_EPS = 1e-6

LANES = 128
SUBLANES = 8
MXU_DIM = 256
N_STATE = SSM_GROUPS * SSM_STATE
VMEM_LIMIT = 56 * 1024 * 1024
MASK_BIAS = -(2.0 ** 100)

NT_DIMS = (((1,), (1,)), ((), ()))


def _dot(a, b):
    return jnp.dot(a, b, preferred_element_type=F32)


def _dot_nt(a, b, precision=None):
    return lax.dot_general(a, b, NT_DIMS, preferred_element_type=F32, precision=precision)


def _rms(x, g):
    return x * lax.rsqrt(jnp.mean(x * x, axis=-1, keepdims=True) + NORM_EPS) * g


def _const_spec(shape):
    nd = len(shape)
    return pl.BlockSpec(shape, lambda *_: (0,) * nd, pipeline_mode=pl.Buffered(1))


def _params(sem, vmem=VMEM_LIMIT):
    return pltpu.CompilerParams(dimension_semantics=sem, vmem_limit_bytes=vmem)


def _ff_chunks(d_ff):
    assert d_ff % MXU_DIM == 0
    n = d_ff // MXU_DIM
    sizes, left = [], n
    while left > 0:
        take = min(4, left)
        sizes.append(take * MXU_DIM)
        left -= take
    return sizes


def _ffn_body(*refs, premix, final, chunks):
    it = iter(refs)
    x_ref = next(it)
    if premix:
        ys_ref, ya_ref, wo_ref = next(it), next(it), next(it)
    g_ref, wg_ref, wu_ref, wd_ref = next(it), next(it), next(it), next(it)
    gf_ref = next(it) if final else None
    o_ref = next(it)

    x = x_ref[...]
    if premix:
        half = ys_ref.shape[1]
        x = x + _dot(ys_ref[...], wo_ref[:half, :]) + _dot(ya_ref[...], wo_ref[half:, :])
    xn = _rms(x, g_ref[...]).astype(BF16)
    acc = jnp.zeros(x.shape, F32)
    off = 0
    for sz in chunks:
        g = _dot(xn, wg_ref[:, off:off + sz])
        u = _dot(xn, wu_ref[:, off:off + sz])
        h = (jax.nn.silu(g) * u).astype(BF16)
        acc = acc + _dot(h, wd_ref[off:off + sz, :])
        off += sz
    y = x + 0.5 * acc
    if final:
        y = _rms(y, gf_ref[...])
    o_ref[...] = y


def _ffn(x, g, wg, wu, wd, *, premix=None, final_g=None, tm):
    m, d = x.shape
    d_ff = wg.shape[1]
    assert m % tm == 0
    row = lambda i: (i, 0)
    args, specs = [x], [pl.BlockSpec((tm, d), row)]
    if premix is not None:
        ys, ya, wo = premix
        args += [ys, ya, wo]
        specs += [pl.BlockSpec((tm, ys.shape[1]), row), pl.BlockSpec((tm, ya.shape[1]), row),
                  _const_spec(wo.shape)]
    args += [g, wg, wu, wd]
    specs += [_const_spec(g.shape), _const_spec(wg.shape), _const_spec(wu.shape), _const_spec(wd.shape)]
    if final_g is not None:
        args.append(final_g)
        specs.append(_const_spec(final_g.shape))
    body = functools.partial(_ffn_body, premix=premix is not None, final=final_g is not None,
                             chunks=_ff_chunks(d_ff))
    return pl.pallas_call(
        body, grid=(m // tm,), in_specs=specs, out_specs=pl.BlockSpec((tm, d), row),
        out_shape=jax.ShapeDtypeStruct((m, d), F32), compiler_params=_params(("parallel",)),
        name="ffn")(*args)


def _rope_tables(positions):
    half = ROT_DIM // 2
    pos = np.asarray(positions, np.float64)
    inv = np.power(ROPE_THETA, -np.arange(half, dtype=np.float64) * 2.0 / ROT_DIM)
    ang = pos[:, None] * inv[None, :]
    cos, sin = np.cos(ang), np.sin(ang)
    d = np.arange(LANES) % HEAD_DIM
    f = d % half
    c = np.where(d[None, :] < ROT_DIM, cos[:, f], 1.0)
    s_next = np.where(d[None, :] < half, -sin[:, f], 0.0)
    s_prev = np.where((d[None, :] >= half) & (d[None, :] < ROT_DIM), sin[:, f], 0.0)
    tok = tuple(jnp.asarray(t, F32) for t in (c, s_next, s_prev))
    feat = (jnp.asarray(cos.T, F32), jnp.asarray(sin.T, F32))
    return tok, feat


def _proj_body(x_ref, g_ref, wuq_ref, wkv_ref, c_ref, sn_ref, sp_ref, ct_ref, st_ref, *rest):
    u_ref, q_ref, kt_ref, vt_ref = rest[-4:]
    half = ROT_DIM // 2
    hb = _rms(x_ref[...], g_ref[...]).astype(BF16)
    uq = _dot(hb, wuq_ref[...])
    u_ref[...] = uq[:, :SSM_WIDTH]
    c, sn, sp = c_ref[...], sn_ref[...], sp_ref[...]
    for j in range(ATTN_WIDTH // LANES):
        qc = uq[:, SSM_WIDTH + j * LANES:SSM_WIDTH + (j + 1) * LANES]
        q_ref[:, j * LANES:(j + 1) * LANES] = (
            qc * c + pltpu.roll(qc, LANES - half, axis=1) * sn + pltpu.roll(qc, half, axis=1) * sp)
    kv = _dot_nt(wkv_ref[...], hb)
    ct, st = ct_ref[...], st_ref[...]
    for h in range(N_HEADS):
        r = h * HEAD_DIM
        a, b = kv[r:r + half], kv[r + half:r + ROT_DIM]
        kt_ref[r:r + half, :] = a * ct - b * st
        kt_ref[r + half:r + ROT_DIM, :] = b * ct + a * st
        kt_ref[r + ROT_DIM:r + HEAD_DIM, :] = kv[r + ROT_DIM:r + HEAD_DIM]
    vt_ref[...] = kv[ATTN_WIDTH:]


def _proj(x3, g, wuq, wkv_t, tok_tabs, feat_tabs, *, tm, layer, n_layers, prev=None):
    bn, sn, d = x3.shape
    assert sn % tm == 0
    tok_spec = pl.BlockSpec((tm, LANES), lambda b, s: (s, 0))
    feat_spec = pl.BlockSpec((ROT_DIM // 2, tm), lambda b, s: (0, s))
    in_specs = [pl.BlockSpec((None, tm, d), lambda b, s: (b, s, 0)), _const_spec(g.shape),
                _const_spec(wuq.shape), _const_spec(wkv_t.shape),
                tok_spec, tok_spec, tok_spec, feat_spec, feat_spec]
    args = [x3, g, wuq, wkv_t, *tok_tabs, *feat_tabs]
    aliases = {}
    if prev is not None:
        in_specs += [pl.BlockSpec(memory_space=pl.ANY)] * 2
        aliases = {len(args): 2, len(args) + 1: 3}
        args += list(prev)
    tok_out = pl.BlockSpec((None, tm, SSM_WIDTH), lambda b, s: (b, s, 0))
    feat_out = pl.BlockSpec((None, None, ATTN_WIDTH, tm), lambda b, s: (layer, b, 0, s))
    stacked = jax.ShapeDtypeStruct((n_layers, bn, ATTN_WIDTH, sn), F32)
    return pl.pallas_call(
        _proj_body, grid=(bn, sn // tm), in_specs=in_specs,
        out_specs=[tok_out, tok_out, feat_out, feat_out],
        out_shape=[jax.ShapeDtypeStruct((bn, sn, SSM_WIDTH), F32),
                   jax.ShapeDtypeStruct((bn, sn, ATTN_WIDTH), F32), stacked, stacked],
        input_output_aliases=aliases, compiler_params=_params(("parallel", "parallel")),
        name="proj")(*args)


def _ssm_prep_body(ldt_ref, ar_ref, ai_ref, btr_ref, bti_ref, abr_ref, abi_ref, bbr_ref, bbi_ref):
    dt = jnp.exp(ldt_ref[...])
    ar, ai = ar_ref[...], ai_ref[...]
    mag = jnp.exp(ar * dt)
    abr = mag * jnp.cos(ai * dt)
    abi = mag * jnp.sin(ai * dt)
    abr_ref[...] = abr
    abi_ref[...] = abi
    den = ar * ar + ai * ai
    n_re, n_im = abr - 1.0, abi
    w_re = ((n_re * ar + n_im * ai) / den)[:, None, :]
    w_im = ((n_im * ar - n_re * ai) / den)[:, None, :]
    btr, bti = btr_ref[...], bti_ref[...]
    bbr_ref[...] = w_re * btr - w_im * bti
    bbi_ref[...] = w_re * bti + w_im * btr


def _ssm_prep(log_dt, a_re, a_im, b_re, b_im):
    g, p, h = b_re.shape
    shp = lambda *s: jax.ShapeDtypeStruct(s, F32)
    return pl.pallas_call(
        _ssm_prep_body, out_shape=[shp(g, p), shp(g, p), shp(g, h, p), shp(g, h, p)], name="ssm_prep",
    )(log_dt.reshape(g, 1), a_re, a_im, b_re.transpose(0, 2, 1), b_im.transpose(0, 2, 1))


def _ssm_layouts(abr, abi, bbr_t, bbi_t, c_re, c_im):
    g, h, p = bbr_t.shape
    n_tiles = g // 2
    blk = jnp.concatenate([bbr_t.reshape(n_tiles, 2, h, p), bbi_t.reshape(n_tiles, 2, h, p)], axis=1)
    wbu = jnp.zeros((n_tiles, LANES, 4, p), F32)
    ch_per_tile = LANES // h
    for j in range(n_tiles):
        for k in range(4):
            grp = 2 * j + (k % 2)
            r0 = (grp % ch_per_tile) * h
            wbu = wbu.at[j, r0:r0 + h, k, :].set(blk[j, k])
    wbu = wbu.reshape(n_tiles, LANES, 4 * p).astype(BF16)
    c_rows = jnp.concatenate([c_re.transpose(0, 2, 1).reshape(n_tiles, 2, p, h),
                              -c_im.transpose(0, 2, 1).reshape(n_tiles, 2, p, h)], axis=1)
    cm = jnp.zeros((n_tiles, 4, p, 2, h), F32)
    for k in range(4):
        cm = cm.at[:, k, :, k % 2, :].set(c_rows[:, k])
    cm = cm.reshape(n_tiles, 4 * p, 2 * h)
    halves = []
    tiles_per_half = n_tiles // 2
    for half in range(2):
        full = jnp.zeros((tiles_per_half, 4 * p, tiles_per_half, 2 * h), F32)
        for t in range(tiles_per_half):
            full = full.at[t, :, t, :].set(cm[half * tiles_per_half + t])
        halves.append(full.reshape(tiles_per_half * 4 * p, tiles_per_half * 2 * h))
    cmat = jnp.stack(halves).astype(BF16)
    ar8 = jnp.broadcast_to(abr.reshape(1, g * p), (SUBLANES, g * p))
    ai8 = jnp.broadcast_to(abi.reshape(1, g * p), (SUBLANES, g * p))
    return wbu, cmat, ar8, ai8


def _perm_matrices(t):
    r = SUBLANES * t
    p = np.zeros((r, r), np.float32)
    for b in range(SUBLANES):
        for s in range(t):
            p[s * SUBLANES + b, b * t + s] = 1.0
    return jnp.asarray(p, BF16), jnp.asarray(p.T, BF16)


def _ssm_body(u_ref, h0r_ref, h0i_ref, p_ref, pt_ref, wbu_ref, cm_ref, ar_ref, ai_ref, d_ref, gw_ref,
              gb_ref, y_ref, hr_ref, hi_ref, bu_ref, *, t_steps):
    r = SUBLANES * t_steps
    n_tiles = wbu_ref.shape[0]
    tile_w = 2 * LANES

    @pl.when(pl.program_id(1) == 0)
    def _():
        hr_ref[...] = h0r_ref[...]
        hi_ref[...] = h0i_ref[...]

    u = u_ref[...].reshape(r, SSM_WIDTH)
    u_hi = u.astype(BF16)
    u_lo = (u - u_hi.astype(F32)).astype(BF16)
    perm = p_ref[...]
    uh = _dot(perm, u_hi)
    u_tb = uh + _dot(perm, u_lo)
    ub = uh.astype(BF16)
    tiles_per_lane_tile = (LANES // SSM_GROUP) // 2
    for j in range(n_tiles):
        lt = j // tiles_per_lane_tile
        bu_ref[:, j * tile_w:(j + 1) * tile_w] = _dot(ub[:, lt * LANES:(lt + 1) * LANES], wbu_ref[j])

    def step(t, carry):
        hr, hi = carry
        row = pl.multiple_of(t * SUBLANES, SUBLANES)
        new_r, new_i = [], []
        for j in range(n_tiles):
            cr = slice(j * tile_w, j * tile_w + LANES)
            ci = slice(j * tile_w + LANES, (j + 1) * tile_w)
            cs = slice(j * LANES, (j + 1) * LANES)
            ar, ai = ar_ref[:, cs], ai_ref[:, cs]
            pr, pi = hr[:, cs], hi[:, cs]
            nr = ar * pr - ai * pi + bu_ref[pl.ds(row, SUBLANES), cr]
            ni = ar * pi + ai * pr + bu_ref[pl.ds(row, SUBLANES), ci]
            bu_ref[pl.ds(row, SUBLANES), cr] = nr
            bu_ref[pl.ds(row, SUBLANES), ci] = ni
            new_r.append(nr)
            new_i.append(ni)
        return jnp.concatenate(new_r, axis=1), jnp.concatenate(new_i, axis=1)

    hr, hi = lax.fori_loop(0, t_steps, step, (hr_ref[...], hi_ref[...]))
    hr_ref[...] = hr
    hi_ref[...] = hi

    half_cols = bu_ref.shape[1] // 2
    y = jnp.concatenate(
        [_dot(bu_ref[:, :half_cols].astype(BF16), cm_ref[0]),
         _dot(bu_ref[:, half_cols:].astype(BF16), cm_ref[1])], axis=1)
    y = y + d_ref[...] * u_tb
    z = jax.nn.gelu(y)
    gate = jax.nn.sigmoid(_dot(z.astype(BF16), gw_ref[...]) + gb_ref[...])
    o = (z * gate).astype(BF16)
    y_ref[...] = _dot(pt_ref[...], o).astype(BF16).reshape(y_ref.shape)


def _ssm(u3, h0r, h0i, wbu, cmat, ar8, ai8, d, gw, gb, *, t_steps):
    bn, ln, w = u3.shape
    assert bn % SUBLANES == 0 and ln % t_steps == 0
    perm, perm_t = _perm_matrices(t_steps)
    r = SUBLANES * t_steps
    state_spec = pl.BlockSpec((SUBLANES, N_STATE), lambda b, c: (b, 0))
    seq_spec = pl.BlockSpec((SUBLANES, t_steps, w), lambda b, c: (b, c, 0))
    consts = [perm, perm_t, wbu, cmat, ar8, ai8, d, gw, gb]
    return pl.pallas_call(
        functools.partial(_ssm_body, t_steps=t_steps),
        grid=(bn // SUBLANES, ln // t_steps),
        in_specs=[seq_spec, state_spec, state_spec] + [_const_spec(c.shape) for c in consts],
        out_specs=[seq_spec, state_spec, state_spec],
        out_shape=[jax.ShapeDtypeStruct((bn, ln, w), BF16),
                   jax.ShapeDtypeStruct((bn, N_STATE), F32), jax.ShapeDtypeStruct((bn, N_STATE), F32)],
        scratch_shapes=[pltpu.VMEM((r, 2 * N_STATE), F32)],
        compiler_params=_params(("parallel", "arbitrary")), name="ssm",
    )(u3, h0r, h0i, *consts)


def _select_rows(g, valid, n_idx, k):
    g = jnp.where(valid, g, -jnp.inf)
    cnt = jnp.zeros(g.shape, jnp.int32)
    for m in range(g.shape[0]):
        row = g[m:m + 1, :]
        beats = (row > g) | ((row == g) & (m < n_idx))
        cnt = cnt + beats.astype(jnp.int32)
    return (cnt < k) & valid


Q_TILE = 128


def _attn_body(q_ref, kt_ref, vt_ref, o_ref, kaug_ref, vtb_ref, km_ref, *, n_blocks):
    c = pl.program_id(2)
    pair_w = 2 * HEAD_DIM
    lane = lax.broadcasted_iota(jnp.int32, (Q_TILE, pair_w), 1)

    @pl.when(c == 0)
    def _():
        sums = [jnp.sum(kt_ref[:, n * MOBA_BLOCK:(n + 1) * MOBA_BLOCK], axis=1, keepdims=True)
                for n in range(n_blocks)]
        km_t = jnp.concatenate(sums + [jnp.zeros((pair_w, pair_w - n_blocks), F32)], axis=1)
        km = (km_t * (1.0 / MOBA_BLOCK)).T[:SUBLANES]
        l8 = lax.broadcasted_iota(jnp.int32, km.shape, 1)
        km_ref[0] = jnp.where(l8 < HEAD_DIM, km, 0.0)
        km_ref[1] = jnp.where(l8 >= HEAD_DIM, km, 0.0)
        rows = lax.broadcasted_iota(jnp.int32, (HEAD_DIM, MOBA_BLOCK), 0)
        for n in range(n_blocks):
            kb = kt_ref[:, n * MOBA_BLOCK:(n + 1) * MOBA_BLOCK].astype(BF16)
            ind = (rows == n).astype(BF16)
            kaug_ref[0, n] = jnp.concatenate([kb[:HEAD_DIM], ind], axis=0)
            kaug_ref[1, n] = jnp.concatenate([ind, kb[HEAD_DIM:]], axis=0)
            vtb_ref[n] = vt_ref[:, n * MOBA_BLOCK:(n + 1) * MOBA_BLOCK].astype(BF16)

    q = q_ref[...]
    own = (c * Q_TILE) // MOBA_BLOCK
    n_idx = lax.broadcasted_iota(jnp.int32, (SUBLANES, Q_TILE), 0)
    eye = (lax.broadcasted_iota(jnp.int32, (Q_TILE, Q_TILE), 0)
           == lax.broadcasted_iota(jnp.int32, (Q_TILE, Q_TILE), 1)).astype(BF16)
    q_pos = c * Q_TILE + lax.broadcasted_iota(jnp.int32, (Q_TILE, MOBA_BLOCK), 0)
    key_off = lax.broadcasted_iota(jnp.int32, (Q_TILE, MOBA_BLOCK), 1)
    outs = []
    for h in range(2):
        g_t = _dot_nt(km_ref[h], q, precision=lax.Precision.HIGHEST)
        sel = _select_rows(g_t, n_idx < own, n_idx, MOBA_TOPK) | (n_idx == own)
        bias_rows = jnp.where(sel, 0.0, MASK_BIAS)
        pad = jnp.zeros((HEAD_DIM - SUBLANES, Q_TILE), F32)
        zero_half = jnp.zeros((HEAD_DIM, Q_TILE), F32)
        if h == 0:
            b_mat = jnp.concatenate([zero_half, bias_rows, pad], axis=0)
            q_h = jnp.where(lane < HEAD_DIM, q, 0.0)
        else:
            b_mat = jnp.concatenate([bias_rows, pad, zero_half], axis=0)
            q_h = jnp.where(lane >= HEAD_DIM, q, 0.0)
        qa = (q_h * (HEAD_DIM ** -0.5) + _dot_nt(eye, b_mat.astype(BF16))).astype(BF16)

        def blk(n, carry, qa=qa, h=h):
            m, l, acc = carry
            s = _dot(qa, kaug_ref[h, n])
            s = jnp.where(n * MOBA_BLOCK + key_off <= q_pos, s, -jnp.inf)
            m_new = jnp.maximum(m, jnp.max(s, axis=1, keepdims=True))
            alpha = jnp.exp(m - m_new)
            p = jnp.exp(s - m_new)
            l = alpha * l + jnp.sum(p, axis=1, keepdims=True)
            acc = alpha * acc + _dot_nt(p.astype(BF16), vtb_ref[n])
            return m_new, l, acc

        init = (jnp.full((Q_TILE, 1), -jnp.inf, F32), jnp.zeros((Q_TILE, 1), F32),
                jnp.zeros((Q_TILE, pair_w), F32))
        m, l, acc = lax.fori_loop(0, own + 1, blk, init)
        outs.append(acc / l)
    o_ref[...] = jnp.where(lane < HEAD_DIM, outs[0], outs[1]).astype(o_ref.dtype)


def _attn_prompt(q3, kt_all, vt_all, layer):
    bn, sn, w = q3.shape
    pair_w = 2 * HEAD_DIM
    n_blocks = sn // MOBA_BLOCK
    assert sn % MOBA_BLOCK == 0 and MOBA_TOPK <= n_blocks <= SUBLANES
    kv_spec = pl.BlockSpec((None, None, pair_w, sn), lambda b, p, c: (layer, b, p, 0))
    return pl.pallas_call(
        functools.partial(_attn_body, n_blocks=n_blocks),
        grid=(bn, w // pair_w, sn // Q_TILE),
        in_specs=[pl.BlockSpec((None, Q_TILE, pair_w), lambda b, p, c: (b, c, p)), kv_spec, kv_spec],
        out_specs=pl.BlockSpec((None, Q_TILE, pair_w), lambda b, p, c: (b, c, p)),
        out_shape=jax.ShapeDtypeStruct((bn, sn, w), BF16),
        scratch_shapes=[pltpu.VMEM((2, n_blocks, pair_w, MOBA_BLOCK), BF16),
                        pltpu.VMEM((n_blocks, pair_w, MOBA_BLOCK), BF16),
                        pltpu.VMEM((2, SUBLANES, pair_w), F32)],
        compiler_params=_params(("parallel", "parallel", "arbitrary")), name="attn_prompt",
    )(q3, kt_all, vt_all)


BLOCKS_PER_STEP = 8


def _kmean_body(pt_ref, *refs):
    del pt_ref
    pages, o_ref = refs[:-1], refs[-1]
    s = pl.program_id(1)

    @pl.when(s == 0)
    def _():
        o_ref[...] = jnp.zeros(o_ref.shape, F32)

    acc = o_ref[...]
    lane = lax.broadcasted_iota(jnp.int32, acc.shape, 2)
    for i in range(BLOCKS_PER_STEP):
        blk = pages[PAGES_PER_BLOCK * i][...]
        for pp in range(1, PAGES_PER_BLOCK):
            blk = blk + pages[PAGES_PER_BLOCK * i + pp][...]
        red = jnp.sum(blk, axis=-1, keepdims=True)
        acc = jnp.where(lane == s * BLOCKS_PER_STEP + i, red, acc)
    o_ref[...] = acc


def _kmean_sample(cache_t, page_table, layer, n_blocks):
    bn = page_table.shape[0]
    assert n_blocks % BLOCKS_PER_STEP == 0 and n_blocks <= LANES
    n_in = BLOCKS_PER_STEP * PAGES_PER_BLOCK
    page_shape = cache_t.shape[2:]

    n_pages = page_table.shape[1]

    def page_spec(k):
        return pl.BlockSpec((None, None) + page_shape,
                            lambda b, s, pt: (layer, pt[b * n_pages + s * n_in + k], 0, 0, 0))

    out_block = (None,) + page_shape[:2] + (LANES,)
    return pl.pallas_call(
        _kmean_body,
        grid_spec=pltpu.PrefetchScalarGridSpec(
            num_scalar_prefetch=1, grid=(bn, n_blocks // BLOCKS_PER_STEP),
            in_specs=[page_spec(k) for k in range(n_in)],
            out_specs=pl.BlockSpec(out_block, lambda b, s, pt: (b, 0, 0, 0))),
        out_shape=jax.ShapeDtypeStruct((bn,) + page_shape[:2] + (LANES,), F32),
        compiler_params=_params(("parallel", "arbitrary")), name="kmean_sample",
    )(page_table.reshape(-1), *([cache_t] * n_in))


def _select_body(q_ref, ks_ref, o_ref, *, past, n_blocks):
    n_new = q_ref.shape[1]
    lane = lax.broadcasted_iota(jnp.int32, (n_new, LANES), 1)
    q_pos = past + lax.broadcasted_iota(jnp.int32, (n_new, LANES), 0)
    valid = (lane < q_pos // MOBA_BLOCK) & (lane < n_blocks)
    for h in range(N_HEADS):
        km = ks_ref[h] * (1.0 / MOBA_BLOCK)
        g = jnp.dot(q_ref[h], km, preferred_element_type=F32, precision=lax.Precision.HIGHEST)
        g = jnp.where(valid, g, -jnp.inf)
        out = jnp.zeros((n_new, LANES), jnp.int32)
        for j in range(MOBA_TOPK):
            best = jnp.max(g, axis=1, keepdims=True)
            idx = jnp.min(jnp.where(g == best, lane, LANES), axis=1, keepdims=True)
            ok = jnp.sum(jnp.where((lane == idx) & valid, 1, 0), axis=1, keepdims=True)
            out = jnp.where(lane == j, jnp.minimum(idx, n_blocks - 1), out)
            out = jnp.where(lane == MOBA_TOPK + j, ok, out)
            g = jnp.where(lane == idx, -jnp.inf, g)
        o_ref[h] = out


def _select_sample(q4, ksum, past, n_blocks):
    bn, hn, qn, dh = q4.shape
    return pl.pallas_call(
        functools.partial(_select_body, past=past, n_blocks=n_blocks),
        grid=(bn,),
        in_specs=[pl.BlockSpec((None, hn, qn, dh), lambda b: (b, 0, 0, 0)),
                  pl.BlockSpec((None, hn, dh, LANES), lambda b: (b, 0, 0, 0))],
        out_specs=pl.BlockSpec((None, hn, qn, LANES), lambda b: (b, 0, 0, 0)),
        out_shape=jax.ShapeDtypeStruct((bn, hn, qn, LANES), jnp.int32),
        compiler_params=_params(("parallel",)), name="select_sample",
    )(q4, ksum)


def _attn_sample_body(sel_ref, pt_ref, q_ref, kn_ref, vn_ref, *refs, past, n_new):
    del pt_ref
    n_sel = MOBA_TOPK * PAGES_PER_BLOCK
    k_sel, v_sel = refs[:n_sel], refs[n_sel:2 * n_sel]
    k_tail = refs[2 * n_sel:2 * n_sel + PAGES_PER_BLOCK]
    v_tail = refs[2 * n_sel + PAGES_PER_BLOCK:2 * n_sel + 2 * PAGES_PER_BLOCK]
    o_ref = refs[-1]
    b, h, qi = pl.program_id(0), pl.program_id(1), pl.program_id(2)
    n_heads, n_q = pl.num_programs(1), pl.num_programs(2)
    sel_base = ((b * n_heads + h) * n_q + qi) * (2 * MOBA_TOPK)
    scale = HEAD_DIM ** -0.5
    qb = (jnp.broadcast_to(q_ref[pl.ds(qi, 1), :], (SUBLANES, HEAD_DIM)) * scale).astype(BF16)
    q_pos = past + qi
    own_start = (q_pos // MOBA_BLOCK) * MOBA_BLOCK
    lane = lax.broadcasted_iota(jnp.int32, (SUBLANES, PAGE_SIZE), 1)

    scores, values = [], []
    for j in range(MOBA_TOPK):
        ok = sel_ref[sel_base + MOBA_TOPK + j] > 0
        for pp in range(PAGES_PER_BLOCK):
            s = _dot(qb, k_sel[j * PAGES_PER_BLOCK + pp][...].astype(BF16))
            scores.append(jnp.where(ok, s, -jnp.inf))
            values.append(v_sel[j * PAGES_PER_BLOCK + pp])
    tail = PAGES_PER_BLOCK * PAGE_SIZE
    for pp in range(PAGES_PER_BLOCK):
        pos = past - tail + pp * PAGE_SIZE + lane
        s = _dot(qb, k_tail[pp][...].astype(BF16))
        scores.append(jnp.where((pos >= own_start) & (pos <= q_pos), s, -jnp.inf))
        values.append(v_tail[pp])
    new_pos = past + lane
    s = _dot(qb, kn_ref[...].astype(BF16))
    scores.append(jnp.where((lane < n_new) & (new_pos >= own_start) & (new_pos <= q_pos), s, -jnp.inf))
    values.append(vn_ref)

    m = jnp.max(scores[-1], axis=1, keepdims=True)
    for s in scores[:-1]:
        m = jnp.maximum(m, jnp.max(s, axis=1, keepdims=True))
    l = jnp.zeros((SUBLANES, 1), F32)
    acc = jnp.zeros((SUBLANES, HEAD_DIM), F32)
    for s, v_ref in zip(scores, values):
        p = jnp.exp(s - m)
        l = l + jnp.sum(p, axis=1, keepdims=True)
        acc = acc + _dot_nt(p.astype(BF16), v_ref[...].astype(BF16))
    o_ref[pl.ds(qi, 1), :] = (acc / l)[0:1, :]


def _attn_sample(sel, page_table, q4, kn_t, vn_t, ck_t, cv_t, layer, past):
    bn, hn, qn, dh = q4.shape
    n_pages = page_table.shape[1]
    assert qn <= PAGE_SIZE
    pad = ((0, 0), (0, 0), (0, 0), (0, PAGE_SIZE - qn))
    kn_t, vn_t = jnp.pad(kn_t, pad), jnp.pad(vn_t, pad)

    def sel_spec(j, pp):
        def imap(b, h, qi, sel_r, pt_r):
            blk = sel_r[((b * hn + h) * qn + qi) * (2 * MOBA_TOPK) + j]
            page = jnp.minimum(blk * PAGES_PER_BLOCK + pp, n_pages - 1)
            return (layer, pt_r[b * n_pages + page], h, 0, 0)
        return pl.BlockSpec((None, None, None, dh, PAGE_SIZE), imap)

    def tail_spec(pp):
        return pl.BlockSpec(
            (None, None, None, dh, PAGE_SIZE),
            lambda b, h, qi, sel_r, pt_r: (layer, pt_r[b * n_pages + n_pages - PAGES_PER_BLOCK + pp], h, 0, 0))

    bh = lambda shape: pl.BlockSpec((None, None) + shape, lambda b, h, qi, sel_r, pt_r: (b, h, 0, 0))
    sel_specs = [sel_spec(j, pp) for j in range(MOBA_TOPK) for pp in range(PAGES_PER_BLOCK)]
    tail_specs = [tail_spec(pp) for pp in range(PAGES_PER_BLOCK)]
    n_sel = len(sel_specs)
    return pl.pallas_call(
        functools.partial(_attn_sample_body, past=past, n_new=qn),
        grid_spec=pltpu.PrefetchScalarGridSpec(
            num_scalar_prefetch=2, grid=(bn, hn, qn),
            in_specs=[bh((qn, dh)), bh((dh, PAGE_SIZE)), bh((dh, PAGE_SIZE))]
            + sel_specs + sel_specs + tail_specs + tail_specs,
            out_specs=bh((qn, dh))),
        out_shape=jax.ShapeDtypeStruct((bn, hn, qn, dh), F32),
        compiler_params=_params(("parallel", "parallel", "arbitrary")), name="attn_sample",
    )(sel.reshape(-1), page_table.reshape(-1), q4, kn_t, vn_t, *([ck_t] * n_sel), *([cv_t] * n_sel),
      *([ck_t] * PAGES_PER_BLOCK), *([cv_t] * PAGES_PER_BLOCK))


FFN_TM = 512
PROJ_TM = 512
SSM_T_PROMPT = 32


def kernel(x_prompt, x_sample, cache_k, cache_v, state_ssm_re, state_ssm_im, page_table, norm_ffn1, ffn1_w_gate, ffn1_w_up, ffn1_w_down, norm_mix, w_in, ssm_log_dt, ssm_a_re, ssm_a_im, ssm_b_re, ssm_b_im, ssm_c_re, ssm_c_im, ssm_d, glu_w, glu_b, w_out, norm_ffn2, ffn2_w_gate, ffn2_w_up, ffn2_w_down, norm_final):
    depth = norm_ffn1.shape[0]
    bp, sp, d = x_prompt.shape
    bs, ss, _ = x_sample.shape
    n_pages = page_table.shape[1]
    past = n_pages * PAGE_SIZE
    n_blocks_past = past // MOBA_BLOCK
    assert n_blocks_past >= MOBA_TOPK

    ck_t = cache_k.transpose(0, 1, 3, 4, 2)
    cv_t = cache_v.transpose(0, 1, 3, 4, 2)

    tok_p, feat_p = _rope_tables(np.arange(sp))
    pos_s = np.tile(past + np.arange(ss), bs)
    tok_s, feat_s = _rope_tables(pos_s)

    xp = x_prompt.reshape(bp * sp, d)
    xs = x_sample.reshape(bs * ss, d)
    zeros_p = jnp.zeros((bp, N_STATE), F32)
    row = lambda v: v.reshape(1, -1)

    kv_p = None
    hr_p, hi_p, k_s, v_s, hr_s, hi_s = [], [], [], [], [], []
    n_tok_s = bs * ss
    for l in range(depth):
        wg1, wu1, wd1 = ffn1_w_gate[l].astype(BF16), ffn1_w_up[l].astype(BF16), ffn1_w_down[l].astype(BF16)
        wg2, wu2, wd2 = ffn2_w_gate[l].astype(BF16), ffn2_w_up[l].astype(BF16), ffn2_w_down[l].astype(BF16)
        wuq = w_in[l][:, :SSM_WIDTH + ATTN_WIDTH].astype(BF16)
        wkv_t = w_in[l][:, SSM_WIDTH + ATTN_WIDTH:].T.astype(BF16)
        wo = w_out[l].astype(BF16)
        gw = glu_w[l].astype(BF16)
        abr, abi, bbr_t, bbi_t = _ssm_prep(ssm_log_dt[l], ssm_a_re[l], ssm_a_im[l], ssm_b_re[l], ssm_b_im[l])
        wbu, cmat, ar8, ai8 = _ssm_layouts(abr, abi, bbr_t, bbi_t, ssm_c_re[l], ssm_c_im[l])
        ssm_consts = (wbu, cmat, ar8, ai8, row(ssm_d[l]), gw, row(glu_b[l]))
        last = l == depth - 1

        xp = _ffn(xp, row(norm_ffn1[l]), wg1, wu1, wd1, tm=FFN_TM)
        u, q, kt_all, vt_all = _proj(xp.reshape(bp, sp, d), row(norm_mix[l]), wuq, wkv_t, tok_p, feat_p,
                                     tm=PROJ_TM, layer=l, n_layers=depth, prev=kv_p)
        kv_p = (kt_all, vt_all)
        y_ssm, hr, hi = _ssm(u, zeros_p, zeros_p, *ssm_consts, t_steps=SSM_T_PROMPT)
        hr_p.append(hr)
        hi_p.append(hi)
        y_att = _attn_prompt(q, kt_all, vt_all, l)
        xp = _ffn(xp, row(norm_ffn2[l]), wg2, wu2, wd2, tm=FFN_TM,
                  premix=(y_ssm.reshape(bp * sp, SSM_WIDTH), y_att.reshape(bp * sp, ATTN_WIDTH), wo),
                  final_g=row(norm_final) if last else None)

        xs = _ffn(xs, row(norm_ffn1[l]), wg1, wu1, wd1, tm=n_tok_s)
        u, q, kt, vt = _proj(xs.reshape(1, n_tok_s, d), row(norm_mix[l]), wuq, wkv_t, tok_s, feat_s,
                             tm=n_tok_s, layer=0, n_layers=1)
        y_ssm, hr, hi = _ssm(u.reshape(bs, ss, SSM_WIDTH), state_ssm_re[l].reshape(bs, N_STATE),
                             state_ssm_im[l].reshape(bs, N_STATE), *ssm_consts, t_steps=ss)
        hr_s.append(hr)
        hi_s.append(hi)
        kn_t = kt.reshape(N_HEADS, HEAD_DIM, bs, ss).transpose(2, 0, 1, 3)
        vn_t = vt.reshape(N_HEADS, HEAD_DIM, bs, ss).transpose(2, 0, 1, 3)
        k_s.append(kn_t.transpose(0, 3, 1, 2))
        v_s.append(vn_t.transpose(0, 3, 1, 2))
        q4 = q.reshape(bs, ss, N_HEADS, HEAD_DIM).transpose(0, 2, 1, 3)
        ksum = _kmean_sample(ck_t, page_table, l, n_blocks_past)
        sel = _select_sample(q4, ksum, past, n_blocks_past)[..., :2 * MOBA_TOPK]
        att = _attn_sample(sel, page_table, q4, kn_t, vn_t, ck_t, cv_t, l, past)
        y_att = att.transpose(0, 2, 1, 3).reshape(n_tok_s, ATTN_WIDTH).astype(BF16)
        xs = _ffn(xs, row(norm_ffn2[l]), wg2, wu2, wd2, tm=n_tok_s,
                  premix=(y_ssm.reshape(n_tok_s, SSM_WIDTH), y_att, wo),
                  final_g=row(norm_final) if last else None)

    kt_all, vt_all = kv_p
    to_tokens = lambda t: t.reshape(depth, bp, N_HEADS, HEAD_DIM, sp).transpose(0, 1, 4, 2, 3)
    state = lambda hs, bn: jnp.stack(hs).reshape(depth, bn, SSM_GROUPS, SSM_STATE)
    return (xp.reshape(bp, sp, d), xs.reshape(bs, ss, d),
            to_tokens(kt_all), to_tokens(vt_all), state(hr_p, bp), state(hi_p, bp),
            jnp.stack(k_s), jnp.stack(v_s), state(hr_s, bs), state(hi_s, bs))
```

```python
import functools
import math

import numpy as np
import jax
import jax.numpy as jnp
from jax import lax
from jax.experimental import pallas as pl
from jax.experimental.pallas import tpu as pltpu

F32 = jnp.float32
BF16 = jnp.bfloat16

D_MODEL = 1024
SSM_WIDTH = 512
SSM_GROUP = 16
SSM_GROUPS = 32
SSM_STATE = 64
ATTN_WIDTH = 512
HEAD_DIM = 64
N_HEADS = 8
ROT_DIM = 16
ROPE_THETA = 500000.0
MOBA_BLOCK = 256
MOBA_TOPK = 3
PAGE_SIZE = 128
PAGES_PER_BLOCK = MOBA_BLOCK // PAGE_SIZE
D_FF = 2816
NORM_EPS = 1e-6

LANES = 128
SUBLANES = 8
MXU_DIM = 256
N_STATE = SSM_GROUPS * SSM_STATE
VMEM_LIMIT = 56 * 1024 * 1024
MASK_BIAS = -(2.0 ** 100)

NT_DIMS = (((1,), (1,)), ((), ()))


def _dot(a, b):
    return jnp.dot(a, b, preferred_element_type=F32)


def _dot_nt(a, b, precision=None):
    return lax.dot_general(a, b, NT_DIMS, preferred_element_type=F32, precision=precision)


def _rms(x, g):
    return x * lax.rsqrt(jnp.mean(x * x, axis=-1, keepdims=True) + NORM_EPS) * g


def _const_spec(shape):
    nd = len(shape)
    return pl.BlockSpec(shape, lambda *_: (0,) * nd, pipeline_mode=pl.Buffered(1))


def _params(sem, vmem=VMEM_LIMIT):
    return pltpu.CompilerParams(dimension_semantics=sem, vmem_limit_bytes=vmem)


def _ff_chunks(d_ff):
    assert d_ff % MXU_DIM == 0
    n = d_ff // MXU_DIM
    sizes, left = [], n
    while left > 0:
        take = min(4, left)
        sizes.append(take * MXU_DIM)
        left -= take
    return sizes


def _ffn_body(*refs, premix, final, chunks):
    it = iter(refs)
    x_ref = next(it)
    if premix:
        ys_ref, ya_ref, wo_ref = next(it), next(it), next(it)
    g_ref, wg_ref, wu_ref, wd_ref = next(it), next(it), next(it), next(it)
    gf_ref = next(it) if final else None
    o_ref = next(it)

    x = x_ref[...]
    if premix:
        half = ys_ref.shape[1]
        x = x + _dot(ys_ref[...], wo_ref[:half, :]) + _dot(ya_ref[...], wo_ref[half:, :])
    xn = _rms(x, g_ref[...]).astype(BF16)
    acc = jnp.zeros(x.shape, F32)
    off = 0
    for sz in chunks:
        g = _dot(xn, wg_ref[:, off:off + sz])
        u = _dot(xn, wu_ref[:, off:off + sz])
        h = (jax.nn.silu(g) * u).astype(BF16)
        acc = acc + _dot(h, wd_ref[off:off + sz, :])
        off += sz
    y = x + 0.5 * acc
    if final:
        y = _rms(y, gf_ref[...])
    o_ref[...] = y


def _ffn(x, g, wg, wu, wd, *, premix=None, final_g=None, tm):
    m, d = x.shape
    d_ff = wg.shape[1]
    assert m % tm == 0
    row = lambda i: (i, 0)
    args, specs = [x], [pl.BlockSpec((tm, d), row)]
    if premix is not None:
        ys, ya, wo = premix
        args += [ys, ya, wo]
        specs += [pl.BlockSpec((tm, ys.shape[1]), row), pl.BlockSpec((tm, ya.shape[1]), row),
                  _const_spec(wo.shape)]
    args += [g, wg, wu, wd]
    specs += [_const_spec(g.shape), _const_spec(wg.shape), _const_spec(wu.shape), _const_spec(wd.shape)]
    if final_g is not None:
        args.append(final_g)
        specs.append(_const_spec(final_g.shape))
    body = functools.partial(_ffn_body, premix=premix is not None, final=final_g is not None,
                             chunks=_ff_chunks(d_ff))
    return pl.pallas_call(
        body, grid=(m // tm,), in_specs=specs, out_specs=pl.BlockSpec((tm, d), row),
        out_shape=jax.ShapeDtypeStruct((m, d), F32), compiler_params=_params(("parallel",)),
        name="ffn")(*args)


def _rope_tables(positions):
    half = ROT_DIM // 2
    pos = np.asarray(positions, np.float64)
    inv = np.power(ROPE_THETA, -np.arange(half, dtype=np.float64) * 2.0 / ROT_DIM)
    ang = pos[:, None] * inv[None, :]
    cos, sin = np.cos(ang), np.sin(ang)
    d = np.arange(LANES) % HEAD_DIM
    f = d % half
    c = np.where(d[None, :] < ROT_DIM, cos[:, f], 1.0)
    s_next = np.where(d[None, :] < half, -sin[:, f], 0.0)
    s_prev = np.where((d[None, :] >= half) & (d[None, :] < ROT_DIM), sin[:, f], 0.0)
    tok = tuple(jnp.asarray(t, F32) for t in (c, s_next, s_prev))
    feat = (jnp.asarray(cos.T, F32), jnp.asarray(sin.T, F32))
    return tok, feat


def _proj_body(x_ref, g_ref, wuq_ref, wkv_ref, c_ref, sn_ref, sp_ref, ct_ref, st_ref, *rest, layer):
    u_ref, q_ref, kt_ref, vt_ref = rest[-4:]
    if len(kt_ref.shape) == 3:
        for l in range(kt_ref.shape[0]):
            if l != layer:
                kt_ref[l] = jnp.zeros(kt_ref.shape[1:], F32)
                vt_ref[l] = jnp.zeros(vt_ref.shape[1:], F32)
        kt_ref, vt_ref = kt_ref.at[layer], vt_ref.at[layer]
    half = ROT_DIM // 2
    hb = _rms(x_ref[...], g_ref[...]).astype(BF16)
    uq = _dot(hb, wuq_ref[...])
    u_ref[...] = uq[:, :SSM_WIDTH]
    c, sn, sp = c_ref[...], sn_ref[...], sp_ref[...]
    for j in range(ATTN_WIDTH // LANES):
        qc = uq[:, SSM_WIDTH + j * LANES:SSM_WIDTH + (j + 1) * LANES]
        q_ref[:, j * LANES:(j + 1) * LANES] = (
            qc * c + pltpu.roll(qc, LANES - half, axis=1) * sn + pltpu.roll(qc, half, axis=1) * sp)
    kv = _dot_nt(wkv_ref[...], hb)
    ct, st = ct_ref[...], st_ref[...]
    for h in range(N_HEADS):
        r = h * HEAD_DIM
        a, b = kv[r:r + half], kv[r + half:r + ROT_DIM]
        kt_ref[r:r + half, :] = a * ct - b * st
        kt_ref[r + half:r + ROT_DIM, :] = b * ct + a * st
        kt_ref[r + ROT_DIM:r + HEAD_DIM, :] = kv[r + ROT_DIM:r + HEAD_DIM]
    vt_ref[...] = kv[ATTN_WIDTH:]


def _proj(x3, g, wuq, wkv_t, tok_tabs, feat_tabs, *, tm, layer, n_layers, prev=None):
    bn, sn, d = x3.shape
    assert sn % tm == 0
    tok_spec = pl.BlockSpec((tm, LANES), lambda b, s: (s, 0))
    feat_spec = pl.BlockSpec((ROT_DIM // 2, tm), lambda b, s: (0, s))
    in_specs = [pl.BlockSpec((None, tm, d), lambda b, s: (b, s, 0)), _const_spec(g.shape),
                _const_spec(wuq.shape), _const_spec(wkv_t.shape),
                tok_spec, tok_spec, tok_spec, feat_spec, feat_spec]
    args = [x3, g, wuq, wkv_t, *tok_tabs, *feat_tabs]
    aliases = {}
    if prev is not None:
        in_specs += [pl.BlockSpec(memory_space=pl.ANY)] * 2
        aliases = {len(args): 2, len(args) + 1: 3}
        args += list(prev)
    tok_out = pl.BlockSpec((None, tm, SSM_WIDTH), lambda b, s: (b, s, 0))
    if prev is None and n_layers > 1:
        feat_out = pl.BlockSpec((n_layers, None, ATTN_WIDTH, tm), lambda b, s: (0, b, 0, s))
    else:
        feat_out = pl.BlockSpec((None, None, ATTN_WIDTH, tm), lambda b, s: (layer, b, 0, s))
    stacked = jax.ShapeDtypeStruct((n_layers, bn, ATTN_WIDTH, sn), F32)
    return pl.pallas_call(
        functools.partial(_proj_body, layer=layer), grid=(bn, sn // tm), in_specs=in_specs,
        out_specs=[tok_out, tok_out, feat_out, feat_out],
        out_shape=[jax.ShapeDtypeStruct((bn, sn, SSM_WIDTH), F32),
                   jax.ShapeDtypeStruct((bn, sn, ATTN_WIDTH), F32), stacked, stacked],
        input_output_aliases=aliases, compiler_params=_params(("parallel", "parallel")),
        name="proj")(*args)


def _ssm_prep_body(ldt_ref, ar_ref, ai_ref, btr_ref, bti_ref, abr_ref, abi_ref, bbr_ref, bbi_ref):
    dt = jnp.exp(ldt_ref[...])
    ar, ai = ar_ref[...], ai_ref[...]
    mag = jnp.exp(ar * dt)
    abr = mag * jnp.cos(ai * dt)
    abi = mag * jnp.sin(ai * dt)
    abr_ref[...] = abr
    abi_ref[...] = abi
    den = ar * ar + ai * ai
    n_re, n_im = abr - 1.0, abi
    w_re = ((n_re * ar + n_im * ai) / den)[:, None, :]
    w_im = ((n_im * ar - n_re * ai) / den)[:, None, :]
    btr, bti = btr_ref[...], bti_ref[...]
    bbr_ref[...] = w_re * btr - w_im * bti
    bbi_ref[...] = w_re * bti + w_im * btr


def _ssm_prep(log_dt, a_re, a_im, b_re, b_im):
    g, p, h = b_re.shape
    shp = lambda *s: jax.ShapeDtypeStruct(s, F32)
    return pl.pallas_call(
        _ssm_prep_body, out_shape=[shp(g, p), shp(g, p), shp(g, h, p), shp(g, h, p)], name="ssm_prep",
    )(log_dt.reshape(g, 1), a_re, a_im, b_re.transpose(0, 2, 1), b_im.transpose(0, 2, 1))


def _ssm_layouts(abr, abi, bbr_t, bbi_t, c_re, c_im):
    g, h, p = bbr_t.shape
    n_tiles = g // 2
    blk = jnp.concatenate([bbr_t.reshape(n_tiles, 2, h, p), bbi_t.reshape(n_tiles, 2, h, p)], axis=1)
    grp_per_lane_tile = LANES // h
    place = np.zeros((n_tiles, 4, grp_per_lane_tile), np.float32)
    for j in range(n_tiles):
        for k in range(4):
            place[j, k, (2 * j + k % 2) % grp_per_lane_tile] = 1.0
    wbu = blk[:, :, None, :, :] * jnp.asarray(place)[:, :, :, None, None]
    wbu = wbu.transpose(0, 2, 3, 1, 4).reshape(n_tiles, LANES, 4 * p).astype(BF16)
    c_rows = jnp.concatenate([c_re.transpose(0, 2, 1).reshape(n_tiles, 2, p, h),
                              -c_im.transpose(0, 2, 1).reshape(n_tiles, 2, p, h)], axis=1)
    tiles_per_half = n_tiles // 2
    diag = np.zeros((tiles_per_half, 4, tiles_per_half, 2), np.float32)
    for t in range(tiles_per_half):
        for k in range(4):
            diag[t, k, t, k % 2] = 1.0
    cmat = (c_rows.reshape(2, tiles_per_half, 4, p, 1, 1, h)
            * jnp.asarray(diag)[None, :, :, None, :, :, None])
    cmat = cmat.reshape(2, tiles_per_half * 4 * p, tiles_per_half * 2 * h).astype(BF16)
    ar8 = jnp.broadcast_to(abr.reshape(1, g * p), (SUBLANES, g * p))
    ai8 = jnp.broadcast_to(abi.reshape(1, g * p), (SUBLANES, g * p))
    return wbu, cmat, ar8, ai8


def _perm_matrices(t):
    r = SUBLANES * t
    p = np.zeros((r, r), np.float32)
    for b in range(SUBLANES):
        for s in range(t):
            p[s * SUBLANES + b, b * t + s] = 1.0
    return jnp.asarray(p, BF16), jnp.asarray(p.T, BF16)


def _ssm_body(u_ref, h0r_ref, h0i_ref, p_ref, pt_ref, wbu_ref, cm_ref, ar_ref, ai_ref, d_ref, gw_ref,
              gb_ref, y_ref, hr_ref, hi_ref, bu_ref, *, t_steps):
    r = SUBLANES * t_steps
    n_tiles = wbu_ref.shape[0]
    tile_w = 2 * LANES

    @pl.when(pl.program_id(1) == 0)
    def _():
        hr_ref[...] = h0r_ref[...]
        hi_ref[...] = h0i_ref[...]

    u = u_ref[...].reshape(r, SSM_WIDTH)
    u_hi = u.astype(BF16)
    u_lo = (u - u_hi.astype(F32)).astype(BF16)
    perm = p_ref[...]
    uh = _dot(perm, u_hi)
    u_tb = uh + _dot(perm, u_lo)
    ub = uh.astype(BF16)
    tiles_per_lane_tile = (LANES // SSM_GROUP) // 2
    for j in range(n_tiles):
        lt = j // tiles_per_lane_tile
        bu_ref[:, j * tile_w:(j + 1) * tile_w] = _dot(ub[:, lt * LANES:(lt + 1) * LANES], wbu_ref[j])

    def step(t, carry):
        hr, hi = carry
        row = pl.multiple_of(t * SUBLANES, SUBLANES)
        new_r, new_i = [], []
        for j in range(n_tiles):
            cr = slice(j * tile_w, j * tile_w + LANES)
            ci = slice(j * tile_w + LANES, (j + 1) * tile_w)
            cs = slice(j * LANES, (j + 1) * LANES)
            ar, ai = ar_ref[:, cs], ai_ref[:, cs]
            pr, pi = hr[:, cs], hi[:, cs]
            nr = ar * pr - ai * pi + bu_ref[pl.ds(row, SUBLANES), cr]
            ni = ar * pi + ai * pr + bu_ref[pl.ds(row, SUBLANES), ci]
            bu_ref[pl.ds(row, SUBLANES), cr] = nr
            bu_ref[pl.ds(row, SUBLANES), ci] = ni
            new_r.append(nr)
            new_i.append(ni)
        return jnp.concatenate(new_r, axis=1), jnp.concatenate(new_i, axis=1)

    hr, hi = lax.fori_loop(0, t_steps, step, (hr_ref[...], hi_ref[...]))
    hr_ref[...] = hr
    hi_ref[...] = hi

    half_cols = bu_ref.shape[1] // 2
    y = jnp.concatenate(
        [_dot(bu_ref[:, :half_cols].astype(BF16), cm_ref[0]),
         _dot(bu_ref[:, half_cols:].astype(BF16), cm_ref[1])], axis=1)
    y = y + d_ref[...] * u_tb
    z = jax.nn.gelu(y)
    gate = jax.nn.sigmoid(_dot(z.astype(BF16), gw_ref[...]) + gb_ref[...])
    o = (z * gate).astype(BF16)
    y_ref[...] = _dot(pt_ref[...], o).astype(BF16).reshape(y_ref.shape)


def _ssm(u3, h0r, h0i, wbu, cmat, ar8, ai8, d, gw, gb, *, t_steps):
    bn, ln, w = u3.shape
    assert bn % SUBLANES == 0 and ln % t_steps == 0
    perm, perm_t = _perm_matrices(t_steps)
    r = SUBLANES * t_steps
    state_spec = pl.BlockSpec((SUBLANES, N_STATE), lambda b, c: (b, 0))
    seq_spec = pl.BlockSpec((SUBLANES, t_steps, w), lambda b, c: (b, c, 0))
    consts = [perm, perm_t, wbu, cmat, ar8, ai8, d, gw, gb]
    return pl.pallas_call(
        functools.partial(_ssm_body, t_steps=t_steps),
        grid=(bn // SUBLANES, ln // t_steps),
        in_specs=[seq_spec, state_spec, state_spec] + [_const_spec(c.shape) for c in consts],
        out_specs=[seq_spec, state_spec, state_spec],
        out_shape=[jax.ShapeDtypeStruct((bn, ln, w), BF16),
                   jax.ShapeDtypeStruct((bn, N_STATE), F32), jax.ShapeDtypeStruct((bn, N_STATE), F32)],
        scratch_shapes=[pltpu.VMEM((r, 2 * N_STATE), F32)],
        compiler_params=_params(("parallel", "arbitrary")), name="ssm",
    )(u3, h0r, h0i, *consts)


def _select_rows(g, valid, n_idx, k):
    g = jnp.where(valid, g, -jnp.inf)
    cnt = jnp.zeros(g.shape, jnp.int32)
    for m in range(g.shape[0]):
        row = g[m:m + 1, :]
        beats = (row > g) | ((row == g) & (m < n_idx))
        cnt = cnt + beats.astype(jnp.int32)
    return (cnt < k) & valid


def _split_bf16(x):
    hi = x.astype(BF16)
    return hi, (x - hi.astype(F32)).astype(BF16)


def _attn_body(q_ref, kt_ref, vt_ref, o_ref, kaug_ref, vtb_ref, s_ref, *, n_blocks):
    pair_w = 2 * HEAD_DIM
    blk = MOBA_BLOCK
    lane = lax.broadcasted_iota(jnp.int32, (blk, pair_w), 1)

    sums = [jnp.sum(kt_ref[:, n * blk:(n + 1) * blk], axis=1, keepdims=True) for n in range(n_blocks)]
    km_t = jnp.concatenate(sums + [jnp.zeros((pair_w, pair_w - n_blocks), F32)], axis=1)
    km = (km_t * (1.0 / blk)).T[:SUBLANES]
    l8 = lax.broadcasted_iota(jnp.int32, km.shape, 1)
    km_split = [_split_bf16(jnp.where(l8 < HEAD_DIM, km, 0.0)), _split_bf16(jnp.where(l8 >= HEAD_DIM, km, 0.0))]
    rows = lax.broadcasted_iota(jnp.int32, (HEAD_DIM, blk), 0)
    for n in range(n_blocks):
        kb = kt_ref[:, n * blk:(n + 1) * blk].astype(BF16)
        ind = (rows == n).astype(BF16)
        kaug_ref[0, n] = jnp.concatenate([kb[:HEAD_DIM], ind], axis=0)
        kaug_ref[1, n] = jnp.concatenate([ind, kb[HEAD_DIM:]], axis=0)
        vtb_ref[n] = vt_ref[:, n * blk:(n + 1) * blk].astype(BF16)

    n_idx = lax.broadcasted_iota(jnp.int32, (SUBLANES, blk), 0)
    eye = (lax.broadcasted_iota(jnp.int32, (blk, blk), 0)
           == lax.broadcasted_iota(jnp.int32, (blk, blk), 1)).astype(BF16)
    causal = (lax.broadcasted_iota(jnp.int32, (blk, blk), 1)
              <= lax.broadcasted_iota(jnp.int32, (blk, blk), 0))
    pad = jnp.zeros((HEAD_DIM - SUBLANES, blk), F32)
    scale = HEAD_DIM ** -0.5

    for t in range(n_blocks):
        q = q_ref[t * blk:(t + 1) * blk, :]
        qs = q * scale
        if t > MOBA_TOPK:
            q_hi, q_lo = _split_bf16(q)
            bias_rows = []
            for h in range(2):
                k_hi, k_lo = km_split[h]
                g_t = _dot_nt(k_hi, q_hi) + _dot_nt(k_hi, q_lo) + _dot_nt(k_lo, q_hi)
                sel = _select_rows(g_t, n_idx < t, n_idx, MOBA_TOPK) | (n_idx == t)
                bias_rows.append(jnp.where(sel, 0.0, MASK_BIAS))
            b_mat = jnp.concatenate([bias_rows[1], pad, bias_rows[0], pad], axis=0).astype(BF16)
            bias_q = _dot_nt(eye, b_mat)
        else:
            bias_q = jnp.zeros((blk, pair_w), F32)
        outs = []
        for h in range(2):
            mine = (lane < HEAD_DIM) if h == 0 else (lane >= HEAD_DIM)
            qa = jnp.where(mine, qs, bias_q).astype(BF16)
            m_run = None
            for n in range(t + 1):
                s = _dot(qa, kaug_ref[h, n])
                if n == t:
                    s = jnp.where(causal, s, -jnp.inf)
                s_ref[h, n] = s
                mx = jnp.maximum(s[:, :LANES], s[:, LANES:])
                m_run = mx if m_run is None else jnp.maximum(m_run, mx)
            m = jnp.broadcast_to(jnp.max(m_run, axis=1, keepdims=True), (blk, LANES))
            l_run = jnp.zeros((blk, LANES), F32)
            acc = jnp.zeros((blk, pair_w), F32)
            for n in range(t + 1):
                p0 = jnp.exp(s_ref[h, n, :, :LANES] - m)
                p1 = jnp.exp(s_ref[h, n, :, LANES:] - m)
                l_run = l_run + (p0 + p1)
                acc = acc + _dot_nt(jnp.concatenate([p0, p1], axis=1).astype(BF16), vtb_ref[n])
            outs.append(acc / jnp.sum(l_run, axis=1, keepdims=True))
        o_ref[t * blk:(t + 1) * blk, :] = jnp.where(lane < HEAD_DIM, outs[0], outs[1]).astype(o_ref.dtype)


def _attn_prompt(q3, kt_all, vt_all, layer):
    bn, sn, w = q3.shape
    pair_w = 2 * HEAD_DIM
    n_blocks = sn // MOBA_BLOCK
    assert sn % MOBA_BLOCK == 0 and MOBA_TOPK <= n_blocks <= SUBLANES
    kv_spec = pl.BlockSpec((None, None, pair_w, sn), lambda b, p: (layer, b, p, 0))
    q_spec = pl.BlockSpec((None, sn, pair_w), lambda b, p: (b, 0, p))
    return pl.pallas_call(
        functools.partial(_attn_body, n_blocks=n_blocks),
        grid=(bn, w // pair_w),
        in_specs=[q_spec, kv_spec, kv_spec], out_specs=q_spec,
        out_shape=jax.ShapeDtypeStruct((bn, sn, w), BF16),
        scratch_shapes=[pltpu.VMEM((2, n_blocks, pair_w, MOBA_BLOCK), BF16),
                        pltpu.VMEM((n_blocks, pair_w, MOBA_BLOCK), BF16),
                        pltpu.VMEM((2, n_blocks, MOBA_BLOCK, MOBA_BLOCK), F32)],
        compiler_params=_params(("parallel", "parallel")), name="attn_prompt",
    )(q3, kt_all, vt_all)


BLOCKS_PER_STEP = 8


def _kmean_body(pt_ref, *refs):
    del pt_ref
    pages, o_ref = refs[:-1], refs[-1]
    s = pl.program_id(1)

    @pl.when(s == 0)
    def _():
        o_ref[...] = jnp.zeros(o_ref.shape, F32)

    acc = o_ref[...]
    lane = lax.broadcasted_iota(jnp.int32, acc.shape, 2)
    for i in range(BLOCKS_PER_STEP):
        blk = pages[PAGES_PER_BLOCK * i][...]
        for pp in range(1, PAGES_PER_BLOCK):
            blk = blk + pages[PAGES_PER_BLOCK * i + pp][...]
        red = jnp.sum(blk, axis=-1, keepdims=True)
        acc = jnp.where(lane == s * BLOCKS_PER_STEP + i, red, acc)
    o_ref[...] = acc


def _kmean_sample(cache_t, page_table, layer, n_blocks):
    bn = page_table.shape[0]
    assert n_blocks % BLOCKS_PER_STEP == 0 and n_blocks <= LANES
    n_in = BLOCKS_PER_STEP * PAGES_PER_BLOCK
    page_shape = cache_t.shape[2:]

    n_pages = page_table.shape[1]

    def page_spec(k):
        return pl.BlockSpec((None, None) + page_shape,
                            lambda b, s, pt: (layer, pt[b * n_pages + s * n_in + k], 0, 0, 0))

    out_block = (None,) + page_shape[:2] + (LANES,)
    return pl.pallas_call(
        _kmean_body,
        grid_spec=pltpu.PrefetchScalarGridSpec(
            num_scalar_prefetch=1, grid=(bn, n_blocks // BLOCKS_PER_STEP),
            in_specs=[page_spec(k) for k in range(n_in)],
            out_specs=pl.BlockSpec(out_block, lambda b, s, pt: (b, 0, 0, 0))),
        out_shape=jax.ShapeDtypeStruct((bn,) + page_shape[:2] + (LANES,), F32),
        compiler_params=_params(("parallel", "arbitrary")), name="kmean_sample",
    )(page_table.reshape(-1), *([cache_t] * n_in))


def _select_body(q_ref, ks_ref, pt_ref, o_ref, *, past, n_blocks, n_pages):
    n_new = q_ref.shape[1]
    lane = lax.broadcasted_iota(jnp.int32, (n_new, LANES), 1)
    q_pos = past + lax.broadcasted_iota(jnp.int32, (n_new, LANES), 0)
    valid = (lane < q_pos // MOBA_BLOCK) & (lane < n_blocks)
    pt_row = jnp.broadcast_to(pt_ref[...], (n_new, n_pages))
    page_lane = lax.broadcasted_iota(jnp.int32, (n_new, n_pages), 1)
    for h in range(N_HEADS):
        km = ks_ref[h] * (1.0 / MOBA_BLOCK)
        g = jnp.dot(q_ref[h], km, preferred_element_type=F32, precision=lax.Precision.HIGHEST)
        g = jnp.where(valid, g, -jnp.inf)
        out = jnp.zeros((n_new, LANES), jnp.int32)
        for j in range(MOBA_TOPK):
            best = jnp.max(g, axis=1, keepdims=True)
            idx = jnp.min(jnp.where(g == best, lane, LANES), axis=1, keepdims=True)
            ok = jnp.sum(jnp.where((lane == idx) & valid, 1, 0), axis=1, keepdims=True)
            out = jnp.where(lane == MOBA_TOPK * PAGES_PER_BLOCK + j, ok, out)
            for pp in range(PAGES_PER_BLOCK):
                page = jnp.minimum(jnp.minimum(idx, n_blocks - 1) * PAGES_PER_BLOCK + pp, n_pages - 1)
                pool = jnp.sum(jnp.where(page_lane == page, pt_row, 0), axis=1, keepdims=True)
                out = jnp.where(lane == j * PAGES_PER_BLOCK + pp, pool, out)
            g = jnp.where(lane == idx, -jnp.inf, g)
        o_ref[h] = out


def _select_sample(q4, ksum, page_table, past, n_blocks):
    bn, hn, qn, dh = q4.shape
    n_pages = page_table.shape[1]
    return pl.pallas_call(
        functools.partial(_select_body, past=past, n_blocks=n_blocks, n_pages=n_pages),
        grid=(bn,),
        in_specs=[pl.BlockSpec((None, hn, qn, dh), lambda b: (b, 0, 0, 0)),
                  pl.BlockSpec((None, hn, dh, LANES), lambda b: (b, 0, 0, 0)),
                  pl.BlockSpec((None, 1, n_pages), lambda b: (b, 0, 0))],
        out_specs=pl.BlockSpec((None, hn, qn, LANES), lambda b: (b, 0, 0, 0)),
        out_shape=jax.ShapeDtypeStruct((bn, hn, qn, LANES), jnp.int32),
        compiler_params=_params(("parallel",)), name="select_sample",
    )(q4, ksum, page_table.reshape(bn, 1, n_pages))


def _attn_sample_body(pages_ref, ok_ref, q_ref, kn_ref, vn_ref, ck_hbm, cv_hbm, o_ref, kbuf, vbuf, sem,
                      *, past, n_new, layer):
    per_q = MOBA_TOPK * PAGES_PER_BLOCK
    n_sel = n_new * per_q
    n_fetch = n_sel + PAGES_PER_BLOCK
    n_heads = pl.num_programs(1)
    step = pl.program_id(0) * n_heads + pl.program_id(1)
    n_steps = pl.num_programs(0) * n_heads
    slot = step % 2

    def copies(st, sl, i):
        page = pages_ref[st * n_fetch + i]
        head = st % n_heads
        return (pltpu.make_async_copy(ck_hbm.at[layer, page, head], kbuf.at[sl, i], sem.at[0, sl]),
                pltpu.make_async_copy(cv_hbm.at[layer, page, head], vbuf.at[sl, i], sem.at[1, sl]))

    def start_all(st, sl):
        for i in range(n_fetch):
            for c in copies(st, sl, i):
                c.start()

    @pl.when(step == 0)
    def _():
        start_all(step, slot)

    @pl.when(step + 1 < n_steps)
    def _():
        start_all(step + 1, 1 - slot)

    for i in range(n_fetch):
        for c in copies(step, slot, i):
            c.wait()

    ok_base = step * n_new * MOBA_TOPK
    scale = HEAD_DIM ** -0.5
    lane = lax.broadcasted_iota(jnp.int32, (SUBLANES, PAGE_SIZE), 1)
    tail = PAGES_PER_BLOCK * PAGE_SIZE
    kt_tail = [kbuf[slot, n_sel + pp].astype(BF16) for pp in range(PAGES_PER_BLOCK)]
    vt_tail = [vbuf[slot, n_sel + pp].astype(BF16) for pp in range(PAGES_PER_BLOCK)]
    kt_new, vt_new = kn_ref[...].astype(BF16), vn_ref[...].astype(BF16)

    out_rows = []
    for qi in range(n_new):
        qb = (jnp.broadcast_to(q_ref[qi:qi + 1, :], (SUBLANES, HEAD_DIM)) * scale).astype(BF16)
        q_pos = past + qi
        own_start = (q_pos // MOBA_BLOCK) * MOBA_BLOCK
        scores, values = [], []
        for j in range(MOBA_TOPK):
            ok = ok_ref[ok_base + qi * MOBA_TOPK + j] > 0
            for pp in range(PAGES_PER_BLOCK):
                idx = qi * per_q + j * PAGES_PER_BLOCK + pp
                s = _dot(qb, kbuf[slot, idx].astype(BF16))
                scores.append(jnp.where(ok, s, -jnp.inf))
                values.append(vbuf[slot, idx].astype(BF16))
        for pp in range(PAGES_PER_BLOCK):
            pos = past - tail + pp * PAGE_SIZE + lane
            s = _dot(qb, kt_tail[pp])
            scores.append(jnp.where((pos >= own_start) & (pos <= q_pos), s, -jnp.inf))
            values.append(vt_tail[pp])
        new_pos = past + lane
        s = _dot(qb, kt_new)
        scores.append(jnp.where((lane < n_new) & (new_pos >= own_start) & (new_pos <= q_pos), s, -jnp.inf))
        values.append(vt_new)

        m_run = scores[0]
        for s in scores[1:]:
            m_run = jnp.maximum(m_run, s)
        m = jnp.max(m_run, axis=1, keepdims=True)
        l_run = jnp.zeros((SUBLANES, PAGE_SIZE), F32)
        acc = jnp.zeros((SUBLANES, HEAD_DIM), F32)
        for s, v in zip(scores, values):
            p = jnp.exp(s - m)
            l_run = l_run + p
            acc = acc + _dot_nt(p.astype(BF16), v)
        out_rows.append((acc / jnp.sum(l_run, axis=1, keepdims=True))[0:1, :])
    o_ref[...] = jnp.concatenate(out_rows, axis=0)


def _attn_sample(sel, page_table, q4, kn_t, vn_t, ck_t, cv_t, layer, past):
    bn, hn, qn, dh = q4.shape
    n_pages = page_table.shape[1]
    assert qn <= PAGE_SIZE
    pad = ((0, 0), (0, 0), (0, 0), (0, PAGE_SIZE - qn))
    kn_t, vn_t = jnp.pad(kn_t, pad), jnp.pad(vn_t, pad)
    per_q = MOBA_TOPK * PAGES_PER_BLOCK
    n_fetch = qn * per_q + PAGES_PER_BLOCK
    tail_pages = jnp.broadcast_to(page_table[:, None, n_pages - PAGES_PER_BLOCK:], (bn, hn, PAGES_PER_BLOCK))
    pages = jnp.concatenate([sel[..., :per_q].reshape(bn, hn, qn * per_q), tail_pages], axis=-1)
    ok = sel[..., per_q:per_q + MOBA_TOPK]

    bh = lambda shape: pl.BlockSpec((None, None) + shape, lambda b, h, pages_r, ok_r: (b, h, 0, 0))
    any_spec = pl.BlockSpec(memory_space=pl.ANY)
    return pl.pallas_call(
        functools.partial(_attn_sample_body, past=past, n_new=qn, layer=layer),
        grid_spec=pltpu.PrefetchScalarGridSpec(
            num_scalar_prefetch=2, grid=(bn, hn),
            in_specs=[bh((qn, dh)), bh((dh, PAGE_SIZE)), bh((dh, PAGE_SIZE)), any_spec, any_spec],
            out_specs=bh((qn, dh)),
            scratch_shapes=[pltpu.VMEM((2, n_fetch, dh, PAGE_SIZE), F32),
                            pltpu.VMEM((2, n_fetch, dh, PAGE_SIZE), F32),
                            pltpu.SemaphoreType.DMA((2, 2))]),
        out_shape=jax.ShapeDtypeStruct((bn, hn, qn, dh), F32),
        compiler_params=_params(("arbitrary", "arbitrary")), name="attn_sample",
    )(pages.reshape(-1), ok.reshape(-1), q4, kn_t, vn_t, ck_t, cv_t)


FFN_TM = 512
PROJ_TM = 512
SSM_T_PROMPT = 64


def kernel(x_prompt, x_sample, cache_k, cache_v, state_ssm_re, state_ssm_im, page_table, norm_ffn1, ffn1_w_gate, ffn1_w_up, ffn1_w_down, norm_mix, w_in, ssm_log_dt, ssm_a_re, ssm_a_im, ssm_b_re, ssm_b_im, ssm_c_re, ssm_c_im, ssm_d, glu_w, glu_b, w_out, norm_ffn2, ffn2_w_gate, ffn2_w_up, ffn2_w_down, norm_final):
    depth = norm_ffn1.shape[0]
    bp, sp, d = x_prompt.shape
    bs, ss, _ = x_sample.shape
    n_pages = page_table.shape[1]
    past = n_pages * PAGE_SIZE
    n_blocks_past = past // MOBA_BLOCK
    assert n_blocks_past >= MOBA_TOPK

    ck_t = cache_k.transpose(0, 1, 3, 4, 2)
    cv_t = cache_v.transpose(0, 1, 3, 4, 2)

    tok_p, feat_p = _rope_tables(np.arange(sp))
    pos_s = np.tile(past + np.arange(ss), bs)
    tok_s, feat_s = _rope_tables(pos_s)

    xp = x_prompt.reshape(bp * sp, d)
    xs = x_sample.reshape(bs * ss, d)
    zeros_p = jnp.zeros((bp, N_STATE), F32)
    row = lambda v: v.reshape(1, -1)

    kv_p = None
    hr_p, hi_p, k_s, v_s, hr_s, hi_s = [], [], [], [], [], []
    n_tok_s = bs * ss
    for l in range(depth):
        wg1, wu1, wd1 = ffn1_w_gate[l].astype(BF16), ffn1_w_up[l].astype(BF16), ffn1_w_down[l].astype(BF16)
        wg2, wu2, wd2 = ffn2_w_gate[l].astype(BF16), ffn2_w_up[l].astype(BF16), ffn2_w_down[l].astype(BF16)
        wuq = w_in[l][:, :SSM_WIDTH + ATTN_WIDTH].astype(BF16)
        wkv_t = w_in[l][:, SSM_WIDTH + ATTN_WIDTH:].T.astype(BF16)
        wo = w_out[l].astype(BF16)
        gw = glu_w[l].astype(BF16)
        abr, abi, bbr_t, bbi_t = _ssm_prep(ssm_log_dt[l], ssm_a_re[l], ssm_a_im[l], ssm_b_re[l], ssm_b_im[l])
        wbu, cmat, ar8, ai8 = _ssm_layouts(abr, abi, bbr_t, bbi_t, ssm_c_re[l], ssm_c_im[l])
        ssm_consts = (wbu, cmat, ar8, ai8, row(ssm_d[l]), gw, row(glu_b[l]))
        last = l == depth - 1

        xp = _ffn(xp, row(norm_ffn1[l]), wg1, wu1, wd1, tm=FFN_TM)
        u, q, kt_all, vt_all = _proj(xp.reshape(bp, sp, d), row(norm_mix[l]), wuq, wkv_t, tok_p, feat_p,
                                     tm=PROJ_TM, layer=l, n_layers=depth, prev=kv_p)
        kv_p = (kt_all, vt_all)
        y_ssm, hr, hi = _ssm(u, zeros_p, zeros_p, *ssm_consts, t_steps=SSM_T_PROMPT)
        hr_p.append(hr)
        hi_p.append(hi)
        y_att = _attn_prompt(q, kt_all, vt_all, l)
        xp = _ffn(xp, row(norm_ffn2[l]), wg2, wu2, wd2, tm=FFN_TM,
                  premix=(y_ssm.reshape(bp * sp, SSM_WIDTH), y_att.reshape(bp * sp, ATTN_WIDTH), wo),
                  final_g=row(norm_final) if last else None)

        xs = _ffn(xs, row(norm_ffn1[l]), wg1, wu1, wd1, tm=n_tok_s)
        u, q, kt, vt = _proj(xs.reshape(1, n_tok_s, d), row(norm_mix[l]), wuq, wkv_t, tok_s, feat_s,
                             tm=n_tok_s, layer=0, n_layers=1)
        y_ssm, hr, hi = _ssm(u.reshape(bs, ss, SSM_WIDTH), state_ssm_re[l].reshape(bs, N_STATE),
                             state_ssm_im[l].reshape(bs, N_STATE), *ssm_consts, t_steps=ss)
        hr_s.append(hr)
        hi_s.append(hi)
        kn_t = kt.reshape(N_HEADS, HEAD_DIM, bs, ss).transpose(2, 0, 1, 3)
        vn_t = vt.reshape(N_HEADS, HEAD_DIM, bs, ss).transpose(2, 0, 1, 3)
        k_s.append(kn_t.transpose(0, 3, 1, 2))
        v_s.append(vn_t.transpose(0, 3, 1, 2))
        q4 = q.reshape(bs, ss, N_HEADS, HEAD_DIM).transpose(0, 2, 1, 3)
        ksum = _kmean_sample(ck_t, page_table, l, n_blocks_past)
        sel = _select_sample(q4, ksum, page_table, past, n_blocks_past)
        att = _attn_sample(sel, page_table, q4, kn_t, vn_t, ck_t, cv_t, l, past)
        y_att = att.transpose(0, 2, 1, 3).reshape(n_tok_s, ATTN_WIDTH).astype(BF16)
        xs = _ffn(xs, row(norm_ffn2[l]), wg2, wu2, wd2, tm=n_tok_s,
                  premix=(y_ssm.reshape(n_tok_s, SSM_WIDTH), y_att, wo),
                  final_g=row(norm_final) if last else None)

    kt_all, vt_all = kv_p
    to_tokens = lambda t: t.reshape(depth, bp, N_HEADS, HEAD_DIM, sp).transpose(0, 1, 4, 2, 3)
    state = lambda hs, bn: jnp.stack(hs).reshape(depth, bn, SSM_GROUPS, SSM_STATE)
    return (xp.reshape(bp, sp, d), xs.reshape(bs, ss, d),
            to_tokens(kt_all), to_tokens(vt_all), state(hr_p, bp), state(hi_p, bp),
            jnp.stack(k_s), jnp.stack(v_s), state(hr_s, bs), state(hi_s, bs))
```

```python
import functools
import math

import numpy as np
import jax
import jax.numpy as jnp
from jax import lax
from jax.experimental import pallas as pl
from jax.experimental.pallas import tpu as pltpu

F32 = jnp.float32
BF16 = jnp.bfloat16

D_MODEL = 1024
SSM_WIDTH = 512
SSM_GROUP = 16
SSM_GROUPS = 32
SSM_STATE = 64
ATTN_WIDTH = 512
HEAD_DIM = 64
N_HEADS = 8
ROT_DIM = 16
ROPE_THETA = 500000.0
MOBA_BLOCK = 256
MOBA_TOPK = 3
PAGE_SIZE = 128
PAGES_PER_BLOCK = MOBA_BLOCK // PAGE_SIZE
D_FF = 2816
NORM_EPS = 1e-6

LANES = 128
SUBLANES = 8
MXU_DIM = 256
N_STATE = SSM_GROUPS * SSM_STATE
VMEM_LIMIT = 56 * 1024 * 1024
MASK_BIAS = -(2.0 ** 100)

NT_DIMS = (((1,), (1,)), ((), ()))


def _dot(a, b):
    return jnp.dot(a, b, preferred_element_type=F32)


def _dot_nt(a, b, precision=None):
    return lax.dot_general(a, b, NT_DIMS, preferred_element_type=F32, precision=precision)


def _rms(x, g):
    return x * lax.rsqrt(jnp.mean(x * x, axis=-1, keepdims=True) + NORM_EPS) * g


def _const_spec(shape):
    nd = len(shape)
    return pl.BlockSpec(shape, lambda *_: (0,) * nd, pipeline_mode=pl.Buffered(1))


def _params(sem, vmem=VMEM_LIMIT):
    return pltpu.CompilerParams(dimension_semantics=sem, vmem_limit_bytes=vmem)


def _ff_chunks(d_ff):
    assert d_ff % MXU_DIM == 0
    n = d_ff // MXU_DIM
    sizes, left = [], n
    while left > 0:
        take = min(4, left)
        sizes.append(take * MXU_DIM)
        left -= take
    return sizes


def _block_sums_step(pt_ref, cache_hbm, ks_ref, pbuf, sem, *, layer, row0, n_pages, steps_per_row):
    pps = pbuf.shape[1]
    bps = pps // PAGES_PER_BLOCK
    step, n_steps = pl.program_id(0), pl.num_programs(0)
    slot = step % 2

    def copies(st, sl):
        base = (row0 + st // steps_per_row) * n_pages + (st % steps_per_row) * pps
        return [pltpu.make_async_copy(cache_hbm.at[layer, pt_ref[base + k]], pbuf.at[sl, k], sem.at[sl])
                for k in range(pps)]

    @pl.when(step == 0)
    def _():
        for c in copies(step, slot):
            c.start()

    @pl.when(step + 1 < n_steps)
    def _():
        for c in copies(step + 1, 1 - slot):
            c.start()

    for c in copies(step, slot):
        c.wait()

    part = step % steps_per_row

    @pl.when(part == 0)
    def _():
        ks_ref[...] = jnp.zeros(ks_ref.shape, F32)

    acc = ks_ref[...]
    lane = lax.broadcasted_iota(jnp.int32, acc.shape, 2)
    for j in range(bps):
        blk = pbuf[slot, PAGES_PER_BLOCK * j]
        for pp in range(1, PAGES_PER_BLOCK):
            blk = blk + pbuf[slot, PAGES_PER_BLOCK * j + pp]
        acc = jnp.where(lane == part * bps + j, jnp.sum(blk, axis=-1, keepdims=True), acc)
    ks_ref[...] = acc


def _ffn_body(*refs, premix, final, chunks, side):
    it = iter(refs)
    pt_ref = next(it) if side else None
    x_ref = next(it)
    if premix:
        ys_ref, ya_ref, wo_ref = next(it), next(it), next(it)
    g_ref, wg_ref, wu_ref, wd_ref = next(it), next(it), next(it), next(it)
    gf_ref = next(it) if final else None
    cache_hbm = next(it) if side else None
    o_ref = next(it)
    if side:
        ks_ref, pbuf, sem = next(it), next(it), next(it)
        _block_sums_step(pt_ref, cache_hbm, ks_ref, pbuf, sem, **side)

    x = x_ref[...]
    if premix:
        half = ys_ref.shape[1]
        x = x + _dot(ys_ref[...], wo_ref[:half, :]) + _dot(ya_ref[...], wo_ref[half:, :])
    xn = _rms(x, g_ref[...]).astype(BF16)
    acc = jnp.zeros(x.shape, F32)
    off = 0
    for sz in chunks:
        g = _dot(xn, wg_ref[:, off:off + sz])
        u = _dot(xn, wu_ref[:, off:off + sz])
        h = (jax.nn.silu(g) * u).astype(BF16)
        acc = acc + _dot(h, wd_ref[off:off + sz, :])
        off += sz
    y = x + 0.5 * acc
    if final:
        y = _rms(y, gf_ref[...])
    o_ref[...] = y


def _ffn(x, g, wg, wu, wd, *, premix=None, final_g=None, tm, block_sums=None):
    m, d = x.shape
    d_ff = wg.shape[1]
    assert m % tm == 0
    n_steps = m // tm
    row = lambda i, *_: (i, 0)
    const = lambda a: pl.BlockSpec(a.shape, lambda *_: (0,) * a.ndim, pipeline_mode=pl.Buffered(1))
    args, specs = [x], [pl.BlockSpec((tm, d), row)]
    if premix is not None:
        ys, ya, wo = premix
        args += [ys, ya, wo]
        specs += [pl.BlockSpec((tm, ys.shape[1]), row), pl.BlockSpec((tm, ya.shape[1]), row), const(wo)]
    args += [g, wg, wu, wd]
    specs += [const(g), const(wg), const(wu), const(wd)]
    if final_g is not None:
        args.append(final_g)
        specs.append(const(final_g))
    out_specs = pl.BlockSpec((tm, d), row)
    out_shape = jax.ShapeDtypeStruct((m, d), F32)
    side, scratch, prefetch = None, [], []
    if block_sums is not None:
        page_table, cache_t, layer, row0, n_rows = block_sums
        n_pages = page_table.shape[1]
        assert n_steps % n_rows == 0
        steps_per_row = n_steps // n_rows
        assert n_pages % (steps_per_row * PAGES_PER_BLOCK) == 0 and n_pages // PAGES_PER_BLOCK <= LANES
        pps = n_pages // steps_per_row
        page_shape = cache_t.shape[2:]
        side = dict(layer=layer, row0=row0, n_pages=n_pages, steps_per_row=steps_per_row)
        prefetch = [page_table.reshape(-1)]
        args.append(cache_t)
        specs.append(pl.BlockSpec(memory_space=pl.ANY))
        ks_block = page_shape[:2] + (LANES,)
        out_specs = [out_specs, pl.BlockSpec((None,) + ks_block, lambda i, *_: (i // steps_per_row, 0, 0, 0))]
        out_shape = [out_shape, jax.ShapeDtypeStruct((n_rows,) + ks_block, F32)]
        scratch = [pltpu.VMEM((2, pps) + page_shape, F32), pltpu.SemaphoreType.DMA((2,))]
    body = functools.partial(_ffn_body, premix=premix is not None, final=final_g is not None,
                             chunks=_ff_chunks(d_ff), side=side)
    return pl.pallas_call(
        body,
        grid_spec=pltpu.PrefetchScalarGridSpec(
            num_scalar_prefetch=len(prefetch), grid=(n_steps,), in_specs=specs, out_specs=out_specs,
            scratch_shapes=scratch),
        out_shape=out_shape,
        compiler_params=_params(("arbitrary",) if side else ("parallel",)), name="ffn")(*prefetch, *args)


def _rope_tables(positions):
    half = ROT_DIM // 2
    pos = np.asarray(positions, np.float64)
    inv = np.power(ROPE_THETA, -np.arange(half, dtype=np.float64) * 2.0 / ROT_DIM)
    ang = pos[:, None] * inv[None, :]
    cos, sin = np.cos(ang), np.sin(ang)
    d = np.arange(LANES) % HEAD_DIM
    f = d % half
    c = np.where(d[None, :] < ROT_DIM, cos[:, f], 1.0)
    s_next = np.where(d[None, :] < half, -sin[:, f], 0.0)
    s_prev = np.where((d[None, :] >= half) & (d[None, :] < ROT_DIM), sin[:, f], 0.0)
    tok = tuple(jnp.asarray(t, F32) for t in (c, s_next, s_prev))
    feat = (jnp.asarray(cos.T, F32), jnp.asarray(sin.T, F32))
    return tok, feat


def _proj_body(x_ref, g_ref, wuq_ref, wkv_ref, c_ref, sn_ref, sp_ref, ct_ref, st_ref, *rest, layer):
    u_ref, q_ref, kt_ref, vt_ref = rest[-4:]
    if len(kt_ref.shape) == 3:
        for l in range(kt_ref.shape[0]):
            if l != layer:
                kt_ref[l] = jnp.zeros(kt_ref.shape[1:], F32)
                vt_ref[l] = jnp.zeros(vt_ref.shape[1:], F32)
        kt_ref, vt_ref = kt_ref.at[layer], vt_ref.at[layer]
    half = ROT_DIM // 2
    hb = _rms(x_ref[...], g_ref[...]).astype(BF16)
    uq = _dot(hb, wuq_ref[...])
    u_ref[...] = uq[:, :SSM_WIDTH]
    c, sn, sp = c_ref[...], sn_ref[...], sp_ref[...]
    for j in range(ATTN_WIDTH // LANES):
        qc = uq[:, SSM_WIDTH + j * LANES:SSM_WIDTH + (j + 1) * LANES]
        q_ref[:, j * LANES:(j + 1) * LANES] = (
            qc * c + pltpu.roll(qc, LANES - half, axis=1) * sn + pltpu.roll(qc, half, axis=1) * sp)
    kv = _dot_nt(wkv_ref[...], hb)
    ct, st = ct_ref[...], st_ref[...]
    for h in range(N_HEADS):
        r = h * HEAD_DIM
        a, b = kv[r:r + half], kv[r + half:r + ROT_DIM]
        kt_ref[r:r + half, :] = a * ct - b * st
        kt_ref[r + half:r + ROT_DIM, :] = b * ct + a * st
        kt_ref[r + ROT_DIM:r + HEAD_DIM, :] = kv[r + ROT_DIM:r + HEAD_DIM]
    vt_ref[...] = kv[ATTN_WIDTH:]


def _proj(x3, g, wuq, wkv_t, tok_tabs, feat_tabs, *, tm, layer, n_layers, prev=None):
    bn, sn, d = x3.shape
    assert sn % tm == 0
    tok_spec = pl.BlockSpec((tm, LANES), lambda b, s: (s, 0))
    feat_spec = pl.BlockSpec((ROT_DIM // 2, tm), lambda b, s: (0, s))
    in_specs = [pl.BlockSpec((None, tm, d), lambda b, s: (b, s, 0)), _const_spec(g.shape),
                _const_spec(wuq.shape), _const_spec(wkv_t.shape),
                tok_spec, tok_spec, tok_spec, feat_spec, feat_spec]
    args = [x3, g, wuq, wkv_t, *tok_tabs, *feat_tabs]
    aliases = {}
    if prev is not None:
        in_specs += [pl.BlockSpec(memory_space=pl.ANY)] * 2
        aliases = {len(args): 2, len(args) + 1: 3}
        args += list(prev)
    tok_out = pl.BlockSpec((None, tm, SSM_WIDTH), lambda b, s: (b, s, 0))
    if prev is None and n_layers > 1:
        feat_out = pl.BlockSpec((n_layers, None, ATTN_WIDTH, tm), lambda b, s: (0, b, 0, s))
    else:
        feat_out = pl.BlockSpec((None, None, ATTN_WIDTH, tm), lambda b, s: (layer, b, 0, s))
    stacked = jax.ShapeDtypeStruct((n_layers, bn, ATTN_WIDTH, sn), F32)
    return pl.pallas_call(
        functools.partial(_proj_body, layer=layer), grid=(bn, sn // tm), in_specs=in_specs,
        out_specs=[tok_out, tok_out, feat_out, feat_out],
        out_shape=[jax.ShapeDtypeStruct((bn, sn, SSM_WIDTH), F32),
                   jax.ShapeDtypeStruct((bn, sn, ATTN_WIDTH), F32), stacked, stacked],
        input_output_aliases=aliases, compiler_params=_params(("parallel", "parallel")),
        name="proj")(*args)


def _ssm_prep_body(ldt_ref, ar_ref, ai_ref, btr_ref, bti_ref, abr_ref, abi_ref, bbr_ref, bbi_ref):
    dt = jnp.exp(ldt_ref[...])
    ar, ai = ar_ref[...], ai_ref[...]
    mag = jnp.exp(ar * dt)
    abr = mag * jnp.cos(ai * dt)
    abi = mag * jnp.sin(ai * dt)
    abr_ref[...] = abr
    abi_ref[...] = abi
    den = ar * ar + ai * ai
    n_re, n_im = abr - 1.0, abi
    w_re = ((n_re * ar + n_im * ai) / den)[:, None, :]
    w_im = ((n_im * ar - n_re * ai) / den)[:, None, :]
    btr, bti = btr_ref[...], bti_ref[...]
    bbr_ref[...] = w_re * btr - w_im * bti
    bbi_ref[...] = w_re * bti + w_im * btr


def _ssm_prep(log_dt, a_re, a_im, b_re, b_im):
    g, p, h = b_re.shape
    shp = lambda *s: jax.ShapeDtypeStruct(s, F32)
    return pl.pallas_call(
        _ssm_prep_body, out_shape=[shp(g, p), shp(g, p), shp(g, h, p), shp(g, h, p)], name="ssm_prep",
    )(log_dt.reshape(g, 1), a_re, a_im, b_re.transpose(0, 2, 1), b_im.transpose(0, 2, 1))


def _ssm_layouts(abr, abi, bbr_t, bbi_t, c_re, c_im):
    g, h, p = bbr_t.shape
    n_tiles = g // 2
    blk = jnp.concatenate([bbr_t.reshape(n_tiles, 2, h, p), bbi_t.reshape(n_tiles, 2, h, p)], axis=1)
    grp_per_lane_tile = LANES // h
    place = np.zeros((n_tiles, 4, grp_per_lane_tile), np.float32)
    for j in range(n_tiles):
        for k in range(4):
            place[j, k, (2 * j + k % 2) % grp_per_lane_tile] = 1.0
    wbu = blk[:, :, None, :, :] * jnp.asarray(place)[:, :, :, None, None]
    wbu = wbu.transpose(0, 2, 3, 1, 4).reshape(n_tiles, LANES, 4 * p).astype(BF16)
    c_rows = jnp.concatenate([c_re.transpose(0, 2, 1).reshape(n_tiles, 2, p, h),
                              -c_im.transpose(0, 2, 1).reshape(n_tiles, 2, p, h)], axis=1)
    tiles_per_half = n_tiles // 2
    diag = np.zeros((tiles_per_half, 4, tiles_per_half, 2), np.float32)
    for t in range(tiles_per_half):
        for k in range(4):
            diag[t, k, t, k % 2] = 1.0
    cmat = (c_rows.reshape(2, tiles_per_half, 4, p, 1, 1, h)
            * jnp.asarray(diag)[None, :, :, None, :, :, None])
    cmat = cmat.reshape(2, tiles_per_half * 4 * p, tiles_per_half * 2 * h).astype(BF16)
    ar8 = jnp.broadcast_to(abr.reshape(1, g * p), (SUBLANES, g * p))
    ai8 = jnp.broadcast_to(abi.reshape(1, g * p), (SUBLANES, g * p))
    return wbu, cmat, ar8, ai8


def _perm_matrices(t):
    r = SUBLANES * t
    p = np.zeros((r, r), np.float32)
    for b in range(SUBLANES):
        for s in range(t):
            p[s * SUBLANES + b, b * t + s] = 1.0
    return jnp.asarray(p, BF16), jnp.asarray(p.T, BF16)


def _ssm_body(u_ref, h0r_ref, h0i_ref, p_ref, pt_ref, wbu_ref, cm_ref, ar_ref, ai_ref, d_ref, gw_ref,
              gb_ref, y_ref, hr_ref, hi_ref, bu_ref, *, t_steps):
    r = SUBLANES * t_steps
    n_tiles = wbu_ref.shape[0]
    tile_w = 2 * LANES

    @pl.when(pl.program_id(1) == 0)
    def _():
        hr_ref[...] = h0r_ref[...]
        hi_ref[...] = h0i_ref[...]

    u = u_ref[...].reshape(r, SSM_WIDTH)
    u_hi = u.astype(BF16)
    u_lo = (u - u_hi.astype(F32)).astype(BF16)
    perm = p_ref[...]
    uh = _dot(perm, u_hi)
    u_tb = uh + _dot(perm, u_lo)
    ub = uh.astype(BF16)
    tiles_per_lane_tile = (LANES // SSM_GROUP) // 2
    for j in range(n_tiles):
        lt = j // tiles_per_lane_tile
        bu_ref[:, j * tile_w:(j + 1) * tile_w] = _dot(ub[:, lt * LANES:(lt + 1) * LANES], wbu_ref[j])

    def step(t, carry):
        hr, hi = carry
        row = pl.multiple_of(t * SUBLANES, SUBLANES)
        new_r, new_i = [], []
        for j in range(n_tiles):
            cr = slice(j * tile_w, j * tile_w + LANES)
            ci = slice(j * tile_w + LANES, (j + 1) * tile_w)
            cs = slice(j * LANES, (j + 1) * LANES)
            ar, ai = ar_ref[:, cs], ai_ref[:, cs]
            pr, pi = hr[:, cs], hi[:, cs]
            nr = ar * pr - ai * pi + bu_ref[pl.ds(row, SUBLANES), cr]
            ni = ar * pi + ai * pr + bu_ref[pl.ds(row, SUBLANES), ci]
            bu_ref[pl.ds(row, SUBLANES), cr] = nr
            bu_ref[pl.ds(row, SUBLANES), ci] = ni
            new_r.append(nr)
            new_i.append(ni)
        return jnp.concatenate(new_r, axis=1), jnp.concatenate(new_i, axis=1)

    hr, hi = lax.fori_loop(0, t_steps, step, (hr_ref[...], hi_ref[...]))
    hr_ref[...] = hr
    hi_ref[...] = hi

    half_cols = bu_ref.shape[1] // 2
    y = jnp.concatenate(
        [_dot(bu_ref[:, :half_cols].astype(BF16), cm_ref[0]),
         _dot(bu_ref[:, half_cols:].astype(BF16), cm_ref[1])], axis=1)
    y = y + d_ref[...] * u_tb
    z = jax.nn.gelu(y)
    gate = jax.nn.sigmoid(_dot(z.astype(BF16), gw_ref[...]) + gb_ref[...])
    o = (z * gate).astype(BF16)
    y_ref[...] = _dot(pt_ref[...], o).astype(BF16).reshape(y_ref.shape)


def _ssm(u3, h0r, h0i, wbu, cmat, ar8, ai8, d, gw, gb, *, t_steps):
    bn, ln, w = u3.shape
    assert bn % SUBLANES == 0 and ln % t_steps == 0
    perm, perm_t = _perm_matrices(t_steps)
    r = SUBLANES * t_steps
    state_spec = pl.BlockSpec((SUBLANES, N_STATE), lambda b, c: (b, 0))
    seq_spec = pl.BlockSpec((SUBLANES, t_steps, w), lambda b, c: (b, c, 0))
    consts = [perm, perm_t, wbu, cmat, ar8, ai8, d, gw, gb]
    return pl.pallas_call(
        functools.partial(_ssm_body, t_steps=t_steps),
        grid=(bn // SUBLANES, ln // t_steps),
        in_specs=[seq_spec, state_spec, state_spec] + [_const_spec(c.shape) for c in consts],
        out_specs=[seq_spec, state_spec, state_spec],
        out_shape=[jax.ShapeDtypeStruct((bn, ln, w), BF16),
                   jax.ShapeDtypeStruct((bn, N_STATE), F32), jax.ShapeDtypeStruct((bn, N_STATE), F32)],
        scratch_shapes=[pltpu.VMEM((r, 2 * N_STATE), F32)],
        compiler_params=_params(("parallel", "arbitrary")), name="ssm",
    )(u3, h0r, h0i, *consts)


def _select_rows(g, valid, n_idx, k):
    g = jnp.where(valid, g, -jnp.inf)
    cnt = jnp.zeros(g.shape, jnp.int32)
    for m in range(g.shape[0]):
        row = g[m:m + 1, :]
        beats = (row > g) | ((row == g) & (m < n_idx))
        cnt = cnt + beats.astype(jnp.int32)
    return (cnt < k) & valid


def _split_bf16(x):
    hi = x.astype(BF16)
    return hi, (x - hi.astype(F32)).astype(BF16)


def _attn_body(q_ref, kt_ref, vt_ref, o_ref, kaug_ref, vtb_ref, s_ref, *, n_blocks):
    pair_w = 2 * HEAD_DIM
    blk = MOBA_BLOCK
    lane = lax.broadcasted_iota(jnp.int32, (blk, pair_w), 1)

    sums = [jnp.sum(kt_ref[:, n * blk:(n + 1) * blk], axis=1, keepdims=True) for n in range(n_blocks)]
    km_t = jnp.concatenate(sums + [jnp.zeros((pair_w, pair_w - n_blocks), F32)], axis=1)
    km = (km_t * (1.0 / blk)).T[:SUBLANES]
    l8 = lax.broadcasted_iota(jnp.int32, km.shape, 1)
    km_split = [_split_bf16(jnp.where(l8 < HEAD_DIM, km, 0.0)), _split_bf16(jnp.where(l8 >= HEAD_DIM, km, 0.0))]
    rows = lax.broadcasted_iota(jnp.int32, (HEAD_DIM, blk), 0)
    for n in range(n_blocks):
        kb = kt_ref[:, n * blk:(n + 1) * blk].astype(BF16)
        ind = (rows == n).astype(BF16)
        kaug_ref[0, n] = jnp.concatenate([kb[:HEAD_DIM], ind], axis=0)
        kaug_ref[1, n] = jnp.concatenate([ind, kb[HEAD_DIM:]], axis=0)
        vtb_ref[n] = vt_ref[:, n * blk:(n + 1) * blk].astype(BF16)

    n_idx = lax.broadcasted_iota(jnp.int32, (SUBLANES, blk), 0)
    eye = (lax.broadcasted_iota(jnp.int32, (blk, blk), 0)
           == lax.broadcasted_iota(jnp.int32, (blk, blk), 1)).astype(BF16)
    causal = (lax.broadcasted_iota(jnp.int32, (blk, blk), 1)
              <= lax.broadcasted_iota(jnp.int32, (blk, blk), 0))
    pad = jnp.zeros((HEAD_DIM - SUBLANES, blk), F32)
    scale = HEAD_DIM ** -0.5

    for t in range(n_blocks):
        q = q_ref[t * blk:(t + 1) * blk, :]
        qs = q * scale
        if t > MOBA_TOPK:
            q_hi, q_lo = _split_bf16(q)
            bias_rows = []
            for h in range(2):
                k_hi, k_lo = km_split[h]
                g_t = _dot_nt(k_hi, q_hi) + _dot_nt(k_hi, q_lo) + _dot_nt(k_lo, q_hi)
                sel = _select_rows(g_t, n_idx < t, n_idx, MOBA_TOPK) | (n_idx == t)
                bias_rows.append(jnp.where(sel, 0.0, MASK_BIAS))
            b_mat = jnp.concatenate([bias_rows[1], pad, bias_rows[0], pad], axis=0).astype(BF16)
            bias_q = _dot_nt(eye, b_mat)
        else:
            bias_q = jnp.zeros((blk, pair_w), F32)
        outs = []
        for h in range(2):
            mine = (lane < HEAD_DIM) if h == 0 else (lane >= HEAD_DIM)
            qa = jnp.where(mine, qs, bias_q).astype(BF16)
            m_run = None
            for n in range(t + 1):
                s = _dot(qa, kaug_ref[h, n])
                if n == t:
                    s = jnp.where(causal, s, -jnp.inf)
                s_ref[h, n] = s
                mx = jnp.maximum(s[:, :LANES], s[:, LANES:])
                m_run = mx if m_run is None else jnp.maximum(m_run, mx)
            m = jnp.broadcast_to(jnp.max(m_run, axis=1, keepdims=True), (blk, LANES))
            l_run = jnp.zeros((blk, LANES), F32)
            acc = jnp.zeros((blk, pair_w), F32)
            for n in range(t + 1):
                p0 = jnp.exp(s_ref[h, n, :, :LANES] - m)
                p1 = jnp.exp(s_ref[h, n, :, LANES:] - m)
                l_run = l_run + (p0 + p1)
                acc = acc + _dot_nt(jnp.concatenate([p0, p1], axis=1).astype(BF16), vtb_ref[n])
            outs.append(acc / jnp.sum(l_run, axis=1, keepdims=True))
        o_ref[t * blk:(t + 1) * blk, :] = jnp.where(lane < HEAD_DIM, outs[0], outs[1]).astype(o_ref.dtype)


def _attn_prompt(q3, kt_all, vt_all, layer):
    bn, sn, w = q3.shape
    pair_w = 2 * HEAD_DIM
    n_blocks = sn // MOBA_BLOCK
    assert sn % MOBA_BLOCK == 0 and MOBA_TOPK <= n_blocks <= SUBLANES
    kv_spec = pl.BlockSpec((None, None, pair_w, sn), lambda b, p: (layer, b, p, 0))
    q_spec = pl.BlockSpec((None, sn, pair_w), lambda b, p: (b, 0, p))
    return pl.pallas_call(
        functools.partial(_attn_body, n_blocks=n_blocks),
        grid=(bn, w // pair_w),
        in_specs=[q_spec, kv_spec, kv_spec], out_specs=q_spec,
        out_shape=jax.ShapeDtypeStruct((bn, sn, w), BF16),
        scratch_shapes=[pltpu.VMEM((2, n_blocks, pair_w, MOBA_BLOCK), BF16),
                        pltpu.VMEM((n_blocks, pair_w, MOBA_BLOCK), BF16),
                        pltpu.VMEM((2, n_blocks, MOBA_BLOCK, MOBA_BLOCK), F32)],
        compiler_params=_params(("parallel", "parallel")), name="attn_prompt",
    )(q3, kt_all, vt_all)


def _select_body(q_ref, ks_ref, pt_ref, o_ref, *, past, n_blocks, n_pages):
    n_new = q_ref.shape[1]
    lane = lax.broadcasted_iota(jnp.int32, (n_new, LANES), 1)
    q_pos = past + lax.broadcasted_iota(jnp.int32, (n_new, LANES), 0)
    valid = (lane < q_pos // MOBA_BLOCK) & (lane < n_blocks)
    pt_row = jnp.broadcast_to(pt_ref[...], (n_new, n_pages))
    page_lane = lax.broadcasted_iota(jnp.int32, (n_new, n_pages), 1)
    for h in range(N_HEADS):
        km = ks_ref[h] * (1.0 / MOBA_BLOCK)
        g = jnp.dot(q_ref[h], km, preferred_element_type=F32, precision=lax.Precision.HIGHEST)
        g = jnp.where(valid, g, -jnp.inf)
        out = jnp.zeros((n_new, LANES), jnp.int32)
        for j in range(MOBA_TOPK):
            best = jnp.max(g, axis=1, keepdims=True)
            idx = jnp.min(jnp.where(g == best, lane, LANES), axis=1, keepdims=True)
            ok = jnp.sum(jnp.where((lane == idx) & valid, 1, 0), axis=1, keepdims=True)
            out = jnp.where(lane == MOBA_TOPK * PAGES_PER_BLOCK + j, ok, out)
            for pp in range(PAGES_PER_BLOCK):
                page = jnp.minimum(jnp.minimum(idx, n_blocks - 1) * PAGES_PER_BLOCK + pp, n_pages - 1)
                pool = jnp.sum(jnp.where(page_lane == page, pt_row, 0), axis=1, keepdims=True)
                out = jnp.where(lane == j * PAGES_PER_BLOCK + pp, pool, out)
            g = jnp.where(lane == idx, -jnp.inf, g)
        o_ref[h] = out


def _select_sample(q4, ksum, page_table, past, n_blocks):
    bn, hn, qn, dh = q4.shape
    n_pages = page_table.shape[1]
    return pl.pallas_call(
        functools.partial(_select_body, past=past, n_blocks=n_blocks, n_pages=n_pages),
        grid=(bn,),
        in_specs=[pl.BlockSpec((None, hn, qn, dh), lambda b: (b, 0, 0, 0)),
                  pl.BlockSpec((None, hn, dh, LANES), lambda b: (b, 0, 0, 0)),
                  pl.BlockSpec((None, 1, n_pages), lambda b: (b, 0, 0))],
        out_specs=pl.BlockSpec((None, hn, qn, LANES), lambda b: (b, 0, 0, 0)),
        out_shape=jax.ShapeDtypeStruct((bn, hn, qn, LANES), jnp.int32),
        compiler_params=_params(("parallel",)), name="select_sample",
    )(q4, ksum, page_table.reshape(bn, 1, n_pages))


def _attn_sample_body(pages_ref, ok_ref, q_ref, kn_ref, vn_ref, ck_hbm, cv_hbm, o_ref, kbuf, vbuf, sem,
                      *, past, n_new, layer):
    per_q = MOBA_TOPK * PAGES_PER_BLOCK
    n_sel = n_new * per_q
    n_fetch = n_sel + PAGES_PER_BLOCK
    n_heads = pl.num_programs(1)
    step = pl.program_id(0) * n_heads + pl.program_id(1)
    n_steps = pl.num_programs(0) * n_heads
    slot = step % 2

    def copies(st, sl, i):
        page = pages_ref[st * n_fetch + i]
        head = st % n_heads
        return (pltpu.make_async_copy(ck_hbm.at[layer, page, head], kbuf.at[sl, i], sem.at[0, sl]),
                pltpu.make_async_copy(cv_hbm.at[layer, page, head], vbuf.at[sl, i], sem.at[1, sl]))

    def start_all(st, sl):
        for i in range(n_fetch):
            for c in copies(st, sl, i):
                c.start()

    @pl.when(step == 0)
    def _():
        start_all(step, slot)

    @pl.when(step + 1 < n_steps)
    def _():
        start_all(step + 1, 1 - slot)

    for i in range(n_fetch):
        for c in copies(step, slot, i):
            c.wait()

    ok_base = step * n_new * MOBA_TOPK
    scale = HEAD_DIM ** -0.5
    blk_w = PAGES_PER_BLOCK * PAGE_SIZE
    blk_lane = lax.broadcasted_iota(jnp.int32, (SUBLANES, blk_w), 1)
    new_lane = lax.broadcasted_iota(jnp.int32, (SUBLANES, PAGE_SIZE), 1)

    def block(buf, first):
        return jnp.concatenate([buf[slot, first + pp] for pp in range(PAGES_PER_BLOCK)], axis=1).astype(BF16)

    kt_tail, vt_tail = block(kbuf, n_sel), block(vbuf, n_sel)
    kt_new, vt_new = kn_ref[...].astype(BF16), vn_ref[...].astype(BF16)

    out_rows = []
    for qi in range(n_new):
        qb = (jnp.broadcast_to(q_ref[qi:qi + 1, :], (SUBLANES, HEAD_DIM)) * scale).astype(BF16)
        q_pos = past + qi
        own_start = (q_pos // MOBA_BLOCK) * MOBA_BLOCK
        scores, values = [], []
        for j in range(MOBA_TOPK):
            ok = ok_ref[ok_base + qi * MOBA_TOPK + j] > 0
            first = qi * per_q + j * PAGES_PER_BLOCK
            scores.append(jnp.where(ok, _dot(qb, block(kbuf, first)), -jnp.inf))
            values.append(block(vbuf, first))
        pos = past - blk_w + blk_lane
        scores.append(jnp.where((pos >= own_start) & (pos <= q_pos), _dot(qb, kt_tail), -jnp.inf))
        values.append(vt_tail)
        new_pos = past + new_lane
        s_new = jnp.where((new_lane < n_new) & (new_pos >= own_start) & (new_pos <= q_pos),
                          _dot(qb, kt_new), -jnp.inf)

        m_run = scores[0]
        for s in scores[1:]:
            m_run = jnp.maximum(m_run, s)
        m = jnp.maximum(jnp.max(m_run, axis=1, keepdims=True), jnp.max(s_new, axis=1, keepdims=True))
        p_new = jnp.exp(s_new - m)
        l_run = jnp.zeros((SUBLANES, blk_w), F32)
        acc = _dot_nt(p_new.astype(BF16), vt_new)
        for s, v in zip(scores, values):
            p = jnp.exp(s - m)
            l_run = l_run + p
            acc = acc + _dot_nt(p.astype(BF16), v)
        l = jnp.sum(l_run, axis=1, keepdims=True) + jnp.sum(p_new, axis=1, keepdims=True)
        out_rows.append((acc / l)[0:1, :])
    o_ref[...] = jnp.concatenate(out_rows, axis=0)


def _attn_sample(sel, page_table, q4, kn_t, vn_t, ck_t, cv_t, layer, past):
    bn, hn, qn, dh = q4.shape
    n_pages = page_table.shape[1]
    assert qn <= PAGE_SIZE
    pad = ((0, 0), (0, 0), (0, 0), (0, PAGE_SIZE - qn))
    kn_t, vn_t = jnp.pad(kn_t, pad), jnp.pad(vn_t, pad)
    per_q = MOBA_TOPK * PAGES_PER_BLOCK
    n_fetch = qn * per_q + PAGES_PER_BLOCK
    tail_pages = jnp.broadcast_to(page_table[:, None, n_pages - PAGES_PER_BLOCK:], (bn, hn, PAGES_PER_BLOCK))
    pages = jnp.concatenate([sel[..., :per_q].reshape(bn, hn, qn * per_q), tail_pages], axis=-1)
    ok = sel[..., per_q:per_q + MOBA_TOPK]

    bh = lambda shape: pl.BlockSpec((None, None) + shape, lambda b, h, pages_r, ok_r: (b, h, 0, 0))
    any_spec = pl.BlockSpec(memory_space=pl.ANY)
    return pl.pallas_call(
        functools.partial(_attn_sample_body, past=past, n_new=qn, layer=layer),
        grid_spec=pltpu.PrefetchScalarGridSpec(
            num_scalar_prefetch=2, grid=(bn, hn),
            in_specs=[bh((qn, dh)), bh((dh, PAGE_SIZE)), bh((dh, PAGE_SIZE)), any_spec, any_spec],
            out_specs=bh((qn, dh)),
            scratch_shapes=[pltpu.VMEM((2, n_fetch, dh, PAGE_SIZE), F32),
                            pltpu.VMEM((2, n_fetch, dh, PAGE_SIZE), F32),
                            pltpu.SemaphoreType.DMA((2, 2))]),
        out_shape=jax.ShapeDtypeStruct((bn, hn, qn, dh), F32),
        compiler_params=_params(("arbitrary", "arbitrary")), name="attn_sample",
    )(pages.reshape(-1), ok.reshape(-1), q4, kn_t, vn_t, ck_t, cv_t)


FFN_TM = 512
PROJ_TM = 512
SSM_T_PROMPT = 64


def kernel(x_prompt, x_sample, cache_k, cache_v, state_ssm_re, state_ssm_im, page_table, norm_ffn1, ffn1_w_gate, ffn1_w_up, ffn1_w_down, norm_mix, w_in, ssm_log_dt, ssm_a_re, ssm_a_im, ssm_b_re, ssm_b_im, ssm_c_re, ssm_c_im, ssm_d, glu_w, glu_b, w_out, norm_ffn2, ffn2_w_gate, ffn2_w_up, ffn2_w_down, norm_final):
    depth = norm_ffn1.shape[0]
    bp, sp, d = x_prompt.shape
    bs, ss, _ = x_sample.shape
    n_pages = page_table.shape[1]
    past = n_pages * PAGE_SIZE
    n_blocks_past = past // MOBA_BLOCK
    assert n_blocks_past >= MOBA_TOPK

    ck_t = cache_k.transpose(0, 1, 3, 4, 2)
    cv_t = cache_v.transpose(0, 1, 3, 4, 2)

    tok_p, feat_p = _rope_tables(np.arange(sp))
    pos_s = np.tile(past + np.arange(ss), bs)
    tok_s, feat_s = _rope_tables(pos_s)

    xp = x_prompt.reshape(bp * sp, d)
    xs = x_sample.reshape(bs * ss, d)
    zeros_p = jnp.zeros((bp, N_STATE), F32)
    row = lambda v: v.reshape(1, -1)

    kv_p = None
    hr_p, hi_p, k_s, v_s, hr_s, hi_s = [], [], [], [], [], []
    n_tok_s = bs * ss
    for l in range(depth):
        wg1, wu1, wd1 = ffn1_w_gate[l].astype(BF16), ffn1_w_up[l].astype(BF16), ffn1_w_down[l].astype(BF16)
        wg2, wu2, wd2 = ffn2_w_gate[l].astype(BF16), ffn2_w_up[l].astype(BF16), ffn2_w_down[l].astype(BF16)
        wuq = w_in[l][:, :SSM_WIDTH + ATTN_WIDTH].astype(BF16)
        wkv_t = w_in[l][:, SSM_WIDTH + ATTN_WIDTH:].T.astype(BF16)
        wo = w_out[l].astype(BF16)
        gw = glu_w[l].astype(BF16)
        abr, abi, bbr_t, bbi_t = _ssm_prep(ssm_log_dt[l], ssm_a_re[l], ssm_a_im[l], ssm_b_re[l], ssm_b_im[l])
        wbu, cmat, ar8, ai8 = _ssm_layouts(abr, abi, bbr_t, bbi_t, ssm_c_re[l], ssm_c_im[l])
        ssm_consts = (wbu, cmat, ar8, ai8, row(ssm_d[l]), gw, row(glu_b[l]))
        last = l == depth - 1

        half_s = bs // 2
        xp, ksum_a = _ffn(xp, row(norm_ffn1[l]), wg1, wu1, wd1, tm=FFN_TM,
                          block_sums=(page_table, ck_t, l, 0, half_s))
        u, q, kt_all, vt_all = _proj(xp.reshape(bp, sp, d), row(norm_mix[l]), wuq, wkv_t, tok_p, feat_p,
                                     tm=PROJ_TM, layer=l, n_layers=depth, prev=kv_p)
        kv_p = (kt_all, vt_all)
        y_ssm, hr, hi = _ssm(u, zeros_p, zeros_p, *ssm_consts, t_steps=SSM_T_PROMPT)
        hr_p.append(hr)
        hi_p.append(hi)
        y_att = _attn_prompt(q, kt_all, vt_all, l)
        xp, ksum_b = _ffn(xp, row(norm_ffn2[l]), wg2, wu2, wd2, tm=FFN_TM,
                          premix=(y_ssm.reshape(bp * sp, SSM_WIDTH), y_att.reshape(bp * sp, ATTN_WIDTH), wo),
                          final_g=row(norm_final) if last else None,
                          block_sums=(page_table, ck_t, l, half_s, bs - half_s))
        ksum = jnp.concatenate([ksum_a, ksum_b], axis=0)

        xs = _ffn(xs, row(norm_ffn1[l]), wg1, wu1, wd1, tm=n_tok_s)
        u, q, kt, vt = _proj(xs.reshape(1, n_tok_s, d), row(norm_mix[l]), wuq, wkv_t, tok_s, feat_s,
                             tm=n_tok_s, layer=0, n_layers=1)
        y_ssm, hr, hi = _ssm(u.reshape(bs, ss, SSM_WIDTH), state_ssm_re[l].reshape(bs, N_STATE),
                             state_ssm_im[l].reshape(bs, N_STATE), *ssm_consts, t_steps=ss)
        hr_s.append(hr)
        hi_s.append(hi)
        kn_t = kt.reshape(N_HEADS, HEAD_DIM, bs, ss).transpose(2, 0, 1, 3)
        vn_t = vt.reshape(N_HEADS, HEAD_DIM, bs, ss).transpose(2, 0, 1, 3)
        k_s.append(kn_t.transpose(0, 3, 1, 2))
        v_s.append(vn_t.transpose(0, 3, 1, 2))
        q4 = q.reshape(bs, ss, N_HEADS, HEAD_DIM).transpose(0, 2, 1, 3)
        sel = _select_sample(q4, ksum, page_table, past, n_blocks_past)
        att = _attn_sample(sel, page_table, q4, kn_t, vn_t, ck_t, cv_t, l, past)
        y_att = att.transpose(0, 2, 1, 3).reshape(n_tok_s, ATTN_WIDTH).astype(BF16)
        xs = _ffn(xs, row(norm_ffn2[l]), wg2, wu2, wd2, tm=n_tok_s,
                  premix=(y_ssm.reshape(n_tok_s, SSM_WIDTH), y_att, wo),
                  final_g=row(norm_final) if last else None)

    kt_all, vt_all = kv_p
    to_tokens = lambda t: t.reshape(depth, bp, N_HEADS, HEAD_DIM, sp).transpose(0, 1, 4, 2, 3)
    state = lambda hs, bn: jnp.stack(hs).reshape(depth, bn, SSM_GROUPS, SSM_STATE)
    return (xp.reshape(bp, sp, d), xs.reshape(bs, ss, d),
            to_tokens(kt_all), to_tokens(vt_all), state(hr_p, bp), state(hi_p, bp),
            jnp.stack(k_s), jnp.stack(v_s), state(hr_s, bs), state(hi_s, bs))
```

```python
import functools
import math

import numpy as np
import jax
import jax.numpy as jnp
from jax import lax
from jax.experimental import pallas as pl
from jax.experimental.pallas import tpu as pltpu

F32 = jnp.float32
BF16 = jnp.bfloat16

D_MODEL = 1024
SSM_WIDTH = 512
SSM_GROUP = 16
SSM_GROUPS = 32
SSM_STATE = 64
ATTN_WIDTH = 512
HEAD_DIM = 64
N_HEADS = 8
ROT_DIM = 16
ROPE_THETA = 500000.0
MOBA_BLOCK = 256
MOBA_TOPK = 3
PAGE_SIZE = 128
PAGES_PER_BLOCK = MOBA_BLOCK // PAGE_SIZE
D_FF = 2816
NORM_EPS = 1e-6

LANES = 128
SUBLANES = 8
MXU_DIM = 256
N_STATE = SSM_GROUPS * SSM_STATE
VMEM_LIMIT = 56 * 1024 * 1024
MASK_BIAS = -(2.0 ** 100)

NT_DIMS = (((1,), (1,)), ((), ()))


def _dot(a, b):
    return jnp.dot(a, b, preferred_element_type=F32)


def _dot_nt(a, b, precision=None):
    return lax.dot_general(a, b, NT_DIMS, preferred_element_type=F32, precision=precision)


def _rms(x, g):
    return x * lax.rsqrt(jnp.mean(x * x, axis=-1, keepdims=True) + NORM_EPS) * g


def _const_spec(shape):
    nd = len(shape)
    return pl.BlockSpec(shape, lambda *_: (0,) * nd, pipeline_mode=pl.Buffered(1))


def _layer_spec(stacked, layer):
    nd = stacked.ndim - 1
    return pl.BlockSpec((None,) + stacked.shape[1:], lambda *_: (layer,) + (0,) * nd,
                        pipeline_mode=pl.Buffered(1))


def _params(sem, vmem=VMEM_LIMIT):
    return pltpu.CompilerParams(dimension_semantics=sem, vmem_limit_bytes=vmem)


def _ff_chunks(d_ff):
    assert d_ff % MXU_DIM == 0
    n = d_ff // MXU_DIM
    sizes, left = [], n
    while left > 0:
        take = min(4, left)
        sizes.append(take * MXU_DIM)
        left -= take
    return sizes


def _block_sums_step(pt_ref, cache_hbm, ks_ref, pbuf, sem, *, layer, row0, n_pages, steps_per_row):
    pps = pbuf.shape[1]
    bps = pps // PAGES_PER_BLOCK
    step, n_steps = pl.program_id(0), pl.num_programs(0)
    slot = step % 2

    def copies(st, sl):
        base = (row0 + st // steps_per_row) * n_pages + (st % steps_per_row) * pps
        return [pltpu.make_async_copy(cache_hbm.at[layer, pt_ref[base + k]], pbuf.at[sl, k], sem.at[sl])
                for k in range(pps)]

    @pl.when(step == 0)
    def _():
        for c in copies(step, slot):
            c.start()

    @pl.when(step + 1 < n_steps)
    def _():
        for c in copies(step + 1, 1 - slot):
            c.start()

    for c in copies(step, slot):
        c.wait()

    part = step % steps_per_row

    @pl.when(part == 0)
    def _():
        ks_ref[...] = jnp.zeros(ks_ref.shape, F32)

    acc = ks_ref[...]
    lane = lax.broadcasted_iota(jnp.int32, acc.shape, 2)
    for j in range(bps):
        blk = pbuf[slot, PAGES_PER_BLOCK * j]
        for pp in range(1, PAGES_PER_BLOCK):
            blk = blk + pbuf[slot, PAGES_PER_BLOCK * j + pp]
        acc = jnp.where(lane == part * bps + j, jnp.sum(blk, axis=-1, keepdims=True), acc)
    ks_ref[...] = acc


def _ffn_body(*refs, premix, final, chunks, side):
    it = iter(refs)
    pt_ref = next(it) if side else None
    x_ref = next(it)
    if premix:
        ys_ref, ya_ref, wo_ref = next(it), next(it), next(it)
    g_ref, wg_ref, wu_ref, wd_ref = next(it), next(it), next(it), next(it)
    gf_ref = next(it) if final else None
    cache_hbm = next(it) if side else None
    o_ref = next(it)
    if side:
        ks_ref, pbuf, sem = next(it), next(it), next(it)
        _block_sums_step(pt_ref, cache_hbm, ks_ref, pbuf, sem, **side)

    x = x_ref[...]
    if premix:
        half = ys_ref.shape[1]
        x = x + _dot(ys_ref[...], wo_ref[:half, :]) + _dot(ya_ref[...], wo_ref[half:, :])
    xn = _rms(x, g_ref[...]).astype(BF16)
    acc = jnp.zeros(x.shape, F32)
    off = 0
    for sz in chunks:
        g = _dot(xn, wg_ref[:, off:off + sz])
        u = _dot(xn, wu_ref[:, off:off + sz])
        h = (jax.nn.silu(g) * u).astype(BF16)
        acc = acc + _dot(h, wd_ref[off:off + sz, :])
        off += sz
    y = x + 0.5 * acc
    if final:
        y = _rms(y, gf_ref[...])
    o_ref[...] = y


def _ffn(x, g, wg, wu, wd, layer, *, premix=None, final_g=None, tm, block_sums=None):
    m, d = x.shape
    d_ff = wg.shape[-1]
    assert m % tm == 0
    n_steps = m // tm
    row = lambda i, *_: (i, 0)
    args, specs = [x], [pl.BlockSpec((tm, d), row)]
    if premix is not None:
        ys, ya, wo = premix
        args += [ys, ya, wo]
        specs += [pl.BlockSpec((tm, ys.shape[1]), row), pl.BlockSpec((tm, ya.shape[1]), row),
                  _layer_spec(wo, layer)]
    args += [g, wg, wu, wd]
    specs += [_const_spec(g.shape), _layer_spec(wg, layer), _layer_spec(wu, layer), _layer_spec(wd, layer)]
    if final_g is not None:
        args.append(final_g)
        specs.append(_const_spec(final_g.shape))
    out_specs = pl.BlockSpec((tm, d), row)
    out_shape = jax.ShapeDtypeStruct((m, d), F32)
    side, scratch, prefetch = None, [], []
    if block_sums is not None:
        page_table, cache_t, layer, row0, n_rows = block_sums
        n_pages = page_table.shape[1]
        assert n_steps % n_rows == 0
        steps_per_row = n_steps // n_rows
        assert n_pages % (steps_per_row * PAGES_PER_BLOCK) == 0 and n_pages // PAGES_PER_BLOCK <= LANES
        pps = n_pages // steps_per_row
        page_shape = cache_t.shape[2:]
        side = dict(layer=layer, row0=row0, n_pages=n_pages, steps_per_row=steps_per_row)
        prefetch = [page_table.reshape(-1)]
        args.append(cache_t)
        specs.append(pl.BlockSpec(memory_space=pl.ANY))
        ks_block = page_shape[:2] + (LANES,)
        out_specs = [out_specs, pl.BlockSpec((None,) + ks_block, lambda i, *_: (i // steps_per_row, 0, 0, 0))]
        out_shape = [out_shape, jax.ShapeDtypeStruct((n_rows,) + ks_block, F32)]
        scratch = [pltpu.VMEM((2, pps) + page_shape, F32), pltpu.SemaphoreType.DMA((2,))]
    body = functools.partial(_ffn_body, premix=premix is not None, final=final_g is not None,
                             chunks=_ff_chunks(d_ff), side=side)
    return pl.pallas_call(
        body,
        grid_spec=pltpu.PrefetchScalarGridSpec(
            num_scalar_prefetch=len(prefetch), grid=(n_steps,), in_specs=specs, out_specs=out_specs,
            scratch_shapes=scratch),
        out_shape=out_shape,
        compiler_params=_params(("arbitrary",) if side else ("parallel",)), name="ffn")(*prefetch, *args)


def _rope_tables(positions):
    half = ROT_DIM // 2
    pos = np.asarray(positions, np.float64)
    inv = np.power(ROPE_THETA, -np.arange(half, dtype=np.float64) * 2.0 / ROT_DIM)
    ang = pos[:, None] * inv[None, :]
    cos, sin = np.cos(ang), np.sin(ang)
    d = np.arange(LANES) % HEAD_DIM
    f = d % half
    c = np.where(d[None, :] < ROT_DIM, cos[:, f], 1.0)
    s_next = np.where(d[None, :] < half, -sin[:, f], 0.0)
    s_prev = np.where((d[None, :] >= half) & (d[None, :] < ROT_DIM), sin[:, f], 0.0)
    tok = tuple(jnp.asarray(t, F32) for t in (c, s_next, s_prev))
    feat = (jnp.asarray(cos.T, F32), jnp.asarray(sin.T, F32))
    return tok, feat


def _proj_body(x_ref, g_ref, wuq_ref, wkv_ref, c_ref, sn_ref, sp_ref, ct_ref, st_ref, *rest, layer):
    u_ref, q_ref, kt_ref, vt_ref = rest[-4:]
    if len(kt_ref.shape) == 3:
        for l in range(kt_ref.shape[0]):
            if l != layer:
                kt_ref[l] = jnp.zeros(kt_ref.shape[1:], F32)
                vt_ref[l] = jnp.zeros(vt_ref.shape[1:], F32)
        kt_ref, vt_ref = kt_ref.at[layer], vt_ref.at[layer]
    half = ROT_DIM // 2
    hb = _rms(x_ref[...], g_ref[...]).astype(BF16)
    uq = _dot(hb, wuq_ref[...])
    u_ref[...] = uq[:, :SSM_WIDTH]
    c, sn, sp = c_ref[...], sn_ref[...], sp_ref[...]
    for j in range(ATTN_WIDTH // LANES):
        qc = uq[:, SSM_WIDTH + j * LANES:SSM_WIDTH + (j + 1) * LANES]
        q_ref[:, j * LANES:(j + 1) * LANES] = (
            qc * c + pltpu.roll(qc, LANES - half, axis=1) * sn + pltpu.roll(qc, half, axis=1) * sp)
    kv = _dot_nt(wkv_ref[...], hb)
    ct, st = ct_ref[...], st_ref[...]
    for h in range(N_HEADS):
        r = h * HEAD_DIM
        a, b = kv[r:r + half], kv[r + half:r + ROT_DIM]
        kt_ref[r:r + half, :] = a * ct - b * st
        kt_ref[r + half:r + ROT_DIM, :] = b * ct + a * st
        kt_ref[r + ROT_DIM:r + HEAD_DIM, :] = kv[r + ROT_DIM:r + HEAD_DIM]
    vt_ref[...] = kv[ATTN_WIDTH:]


def _proj(x3, g, wuq, wkv_t, w_layer, tok_tabs, feat_tabs, *, tm, layer, n_layers, prev=None):
    bn, sn, d = x3.shape
    assert sn % tm == 0
    tok_spec = pl.BlockSpec((tm, LANES), lambda b, s: (s, 0))
    feat_spec = pl.BlockSpec((ROT_DIM // 2, tm), lambda b, s: (0, s))
    in_specs = [pl.BlockSpec((None, tm, d), lambda b, s: (b, s, 0)), _const_spec(g.shape),
                _layer_spec(wuq, w_layer), _layer_spec(wkv_t, w_layer),
                tok_spec, tok_spec, tok_spec, feat_spec, feat_spec]
    args = [x3, g, wuq, wkv_t, *tok_tabs, *feat_tabs]
    aliases = {}
    if prev is not None:
        in_specs += [pl.BlockSpec(memory_space=pl.ANY)] * 2
        aliases = {len(args): 2, len(args) + 1: 3}
        args += list(prev)
    tok_out = pl.BlockSpec((None, tm, SSM_WIDTH), lambda b, s: (b, s, 0))
    if prev is None and n_layers > 1:
        feat_out = pl.BlockSpec((n_layers, None, ATTN_WIDTH, tm), lambda b, s: (0, b, 0, s))
    else:
        feat_out = pl.BlockSpec((None, None, ATTN_WIDTH, tm), lambda b, s: (layer, b, 0, s))
    stacked = jax.ShapeDtypeStruct((n_layers, bn, ATTN_WIDTH, sn), F32)
    return pl.pallas_call(
        functools.partial(_proj_body, layer=layer), grid=(bn, sn // tm), in_specs=in_specs,
        out_specs=[tok_out, tok_out, feat_out, feat_out],
        out_shape=[jax.ShapeDtypeStruct((bn, sn, SSM_WIDTH), F32),
                   jax.ShapeDtypeStruct((bn, sn, ATTN_WIDTH), F32), stacked, stacked],
        input_output_aliases=aliases, compiler_params=_params(("parallel", "parallel")),
        name="proj")(*args)


def _ssm_prep_body(ldt_ref, ar_ref, ai_ref, btr_ref, bti_ref, abr_ref, abi_ref, bbr_ref, bbi_ref):
    dt = jnp.exp(ldt_ref[...])
    ar, ai = ar_ref[...], ai_ref[...]
    mag = jnp.exp(ar * dt)
    abr = mag * jnp.cos(ai * dt)
    abi = mag * jnp.sin(ai * dt)
    abr_ref[...] = abr
    abi_ref[...] = abi
    den = ar * ar + ai * ai
    n_re, n_im = abr - 1.0, abi
    w_re = ((n_re * ar + n_im * ai) / den)[:, None, :]
    w_im = ((n_im * ar - n_re * ai) / den)[:, None, :]
    btr, bti = btr_ref[...], bti_ref[...]
    bbr_ref[...] = w_re * btr - w_im * bti
    bbi_ref[...] = w_re * bti + w_im * btr


def _ssm_prep(log_dt, a_re, a_im, b_re, b_im):
    g, p, h = b_re.shape
    shp = lambda *s: jax.ShapeDtypeStruct(s, F32)
    return pl.pallas_call(
        _ssm_prep_body, out_shape=[shp(g, p), shp(g, p), shp(g, h, p), shp(g, h, p)], name="ssm_prep",
    )(log_dt.reshape(g, 1), a_re, a_im, b_re.transpose(0, 2, 1), b_im.transpose(0, 2, 1))


def _ssm_layouts(abr, abi, bbr_t, bbi_t, c_re, c_im):
    g, h, p = bbr_t.shape
    n_tiles = g // 2
    blk = jnp.concatenate([bbr_t.reshape(n_tiles, 2, h, p), bbi_t.reshape(n_tiles, 2, h, p)], axis=1)
    grp_per_lane_tile = LANES // h
    place = np.zeros((n_tiles, 4, grp_per_lane_tile), np.float32)
    for j in range(n_tiles):
        for k in range(4):
            place[j, k, (2 * j + k % 2) % grp_per_lane_tile] = 1.0
    wbu = blk[:, :, None, :, :] * jnp.asarray(place)[:, :, :, None, None]
    wbu = wbu.transpose(0, 2, 3, 1, 4).reshape(n_tiles, LANES, 4 * p).astype(BF16)
    c_rows = jnp.concatenate([c_re.transpose(0, 2, 1).reshape(n_tiles, 2, p, h),
                              -c_im.transpose(0, 2, 1).reshape(n_tiles, 2, p, h)], axis=1)
    tiles_per_half = n_tiles // 2
    diag = np.zeros((tiles_per_half, 4, tiles_per_half, 2), np.float32)
    for t in range(tiles_per_half):
        for k in range(4):
            diag[t, k, t, k % 2] = 1.0
    cmat = (c_rows.reshape(2, tiles_per_half, 4, p, 1, 1, h)
            * jnp.asarray(diag)[None, :, :, None, :, :, None])
    cmat = cmat.reshape(2, tiles_per_half * 4 * p, tiles_per_half * 2 * h).astype(BF16)
    ar8 = jnp.broadcast_to(abr.reshape(1, g * p), (SUBLANES, g * p))
    ai8 = jnp.broadcast_to(abi.reshape(1, g * p), (SUBLANES, g * p))
    return wbu, cmat, ar8, ai8


def _perm_matrices(t):
    r = SUBLANES * t
    p = np.zeros((r, r), np.float32)
    for b in range(SUBLANES):
        for s in range(t):
            p[s * SUBLANES + b, b * t + s] = 1.0
    return jnp.asarray(p, BF16), jnp.asarray(p.T, BF16)


def _ssm_body(u_ref, h0r_ref, h0i_ref, p_ref, pt_ref, wbu_ref, cm_ref, ar_ref, ai_ref, d_ref, gw_ref,
              gb_ref, y_ref, hr_ref, hi_ref, bu_ref, *, t_steps):
    r = SUBLANES * t_steps
    n_tiles = wbu_ref.shape[0]
    tile_w = 2 * LANES

    @pl.when(pl.program_id(1) == 0)
    def _():
        hr_ref[...] = h0r_ref[...]
        hi_ref[...] = h0i_ref[...]

    u = u_ref[...].reshape(r, SSM_WIDTH)
    u_hi = u.astype(BF16)
    u_lo = (u - u_hi.astype(F32)).astype(BF16)
    perm = p_ref[...]
    uh = _dot(perm, u_hi)
    u_tb = uh + _dot(perm, u_lo)
    ub = uh.astype(BF16)
    tiles_per_lane_tile = (LANES // SSM_GROUP) // 2
    for j in range(n_tiles):
        lt = j // tiles_per_lane_tile
        bu_ref[:, j * tile_w:(j + 1) * tile_w] = _dot(ub[:, lt * LANES:(lt + 1) * LANES], wbu_ref[j])

    def step(t, carry):
        hr, hi = carry
        row = pl.multiple_of(t * SUBLANES, SUBLANES)
        new_r, new_i = [], []
        for j in range(n_tiles):
            cr = slice(j * tile_w, j * tile_w + LANES)
            ci = slice(j * tile_w + LANES, (j + 1) * tile_w)
            cs = slice(j * LANES, (j + 1) * LANES)
            ar, ai = ar_ref[:, cs], ai_ref[:, cs]
            pr, pi = hr[:, cs], hi[:, cs]
            nr = ar * pr - ai * pi + bu_ref[pl.ds(row, SUBLANES), cr]
            ni = ar * pi + ai * pr + bu_ref[pl.ds(row, SUBLANES), ci]
            bu_ref[pl.ds(row, SUBLANES), cr] = nr
            bu_ref[pl.ds(row, SUBLANES), ci] = ni
            new_r.append(nr)
            new_i.append(ni)
        return jnp.concatenate(new_r, axis=1), jnp.concatenate(new_i, axis=1)

    hr, hi = lax.fori_loop(0, t_steps, step, (hr_ref[...], hi_ref[...]))
    hr_ref[...] = hr
    hi_ref[...] = hi

    half_cols = bu_ref.shape[1] // 2
    y = jnp.concatenate(
        [_dot(bu_ref[:, :half_cols].astype(BF16), cm_ref[0]),
         _dot(bu_ref[:, half_cols:].astype(BF16), cm_ref[1])], axis=1)
    y = y + d_ref[...] * u_tb
    z = jax.nn.gelu(y)
    gate = jax.nn.sigmoid(_dot(z.astype(BF16), gw_ref[...]) + gb_ref[...])
    o = (z * gate).astype(BF16)
    y_ref[...] = _dot(pt_ref[...], o).astype(BF16).reshape(y_ref.shape)


def _ssm(u3, h0r, h0i, wbu, cmat, ar8, ai8, d, gw, gb, layer, *, t_steps):
    bn, ln, w = u3.shape
    assert bn % SUBLANES == 0 and ln % t_steps == 0
    perm, perm_t = _perm_matrices(t_steps)
    r = SUBLANES * t_steps
    state_spec = pl.BlockSpec((SUBLANES, N_STATE), lambda b, c: (b, 0))
    seq_spec = pl.BlockSpec((SUBLANES, t_steps, w), lambda b, c: (b, c, 0))
    consts = [perm, perm_t, wbu, cmat, ar8, ai8, d, gw, gb]
    return pl.pallas_call(
        functools.partial(_ssm_body, t_steps=t_steps),
        grid=(bn // SUBLANES, ln // t_steps),
        in_specs=[seq_spec, state_spec, state_spec]
        + [_layer_spec(c, layer) if c is gw else _const_spec(c.shape) for c in consts],
        out_specs=[seq_spec, state_spec, state_spec],
        out_shape=[jax.ShapeDtypeStruct((bn, ln, w), BF16),
                   jax.ShapeDtypeStruct((bn, N_STATE), F32), jax.ShapeDtypeStruct((bn, N_STATE), F32)],
        scratch_shapes=[pltpu.VMEM((r, 2 * N_STATE), F32)],
        compiler_params=_params(("parallel", "arbitrary")), name="ssm",
    )(u3, h0r, h0i, *consts)


def _select_rows(g, valid, n_idx, k):
    g = jnp.where(valid, g, -jnp.inf)
    cnt = jnp.zeros(g.shape, jnp.int32)
    for m in range(g.shape[0]):
        row = g[m:m + 1, :]
        beats = (row > g) | ((row == g) & (m < n_idx))
        cnt = cnt + beats.astype(jnp.int32)
    return (cnt < k) & valid


def _split_bf16(x):
    hi = x.astype(BF16)
    return hi, (x - hi.astype(F32)).astype(BF16)


def _attn_body(q_ref, kt_ref, vt_ref, o_ref, kaug_ref, vtb_ref, s_ref, *, n_blocks):
    pair_w = 2 * HEAD_DIM
    blk = MOBA_BLOCK
    lane = lax.broadcasted_iota(jnp.int32, (blk, pair_w), 1)

    sums = [jnp.sum(kt_ref[:, n * blk:(n + 1) * blk], axis=1, keepdims=True) for n in range(n_blocks)]
    km_t = jnp.concatenate(sums + [jnp.zeros((pair_w, pair_w - n_blocks), F32)], axis=1)
    km = (km_t * (1.0 / blk)).T[:SUBLANES]
    l8 = lax.broadcasted_iota(jnp.int32, km.shape, 1)
    km_split = [_split_bf16(jnp.where(l8 < HEAD_DIM, km, 0.0)), _split_bf16(jnp.where(l8 >= HEAD_DIM, km, 0.0))]
    rows = lax.broadcasted_iota(jnp.int32, (HEAD_DIM, blk), 0)
    for n in range(n_blocks):
        kb = kt_ref[:, n * blk:(n + 1) * blk].astype(BF16)
        ind = (rows == n).astype(BF16)
        kaug_ref[0, n] = jnp.concatenate([kb[:HEAD_DIM], ind], axis=0)
        kaug_ref[1, n] = jnp.concatenate([ind, kb[HEAD_DIM:]], axis=0)
        vtb_ref[n] = vt_ref[:, n * blk:(n + 1) * blk].astype(BF16)

    n_idx = lax.broadcasted_iota(jnp.int32, (SUBLANES, blk), 0)
    eye = (lax.broadcasted_iota(jnp.int32, (blk, blk), 0)
           == lax.broadcasted_iota(jnp.int32, (blk, blk), 1)).astype(BF16)
    causal = (lax.broadcasted_iota(jnp.int32, (blk, blk), 1)
              <= lax.broadcasted_iota(jnp.int32, (blk, blk), 0))
    pad = jnp.zeros((HEAD_DIM - SUBLANES, blk), F32)
    scale = HEAD_DIM ** -0.5

    for t in range(n_blocks):
        q = q_ref[t * blk:(t + 1) * blk, :]
        qs = q * scale
        if t > MOBA_TOPK:
            q_hi, q_lo = _split_bf16(q)
            bias_rows = []
            for h in range(2):
                k_hi, k_lo = km_split[h]
                g_t = _dot_nt(k_hi, q_hi) + _dot_nt(k_hi, q_lo) + _dot_nt(k_lo, q_hi)
                sel = _select_rows(g_t, n_idx < t, n_idx, MOBA_TOPK) | (n_idx == t)
                bias_rows.append(jnp.where(sel, 0.0, MASK_BIAS))
            b_mat = jnp.concatenate([bias_rows[1], pad, bias_rows[0], pad], axis=0).astype(BF16)
            bias_q = _dot_nt(eye, b_mat)
        else:
            bias_q = jnp.zeros((blk, pair_w), F32)
        outs = []
        for h in range(2):
            mine = (lane < HEAD_DIM) if h == 0 else (lane >= HEAD_DIM)
            qa = jnp.where(mine, qs, bias_q).astype(BF16)
            m_run = None
            for n in range(t + 1):
                s = _dot(qa, kaug_ref[h, n])
                if n == t:
                    s = jnp.where(causal, s, -jnp.inf)
                s_ref[h, n] = s
                mx = jnp.maximum(s[:, :LANES], s[:, LANES:])
                m_run = mx if m_run is None else jnp.maximum(m_run, mx)
            m = jnp.broadcast_to(jnp.max(m_run, axis=1, keepdims=True), (blk, LANES))
            l_run = jnp.zeros((blk, LANES), F32)
            acc = jnp.zeros((blk, pair_w), F32)
            for n in range(t + 1):
                p0 = jnp.exp(s_ref[h, n, :, :LANES] - m)
                p1 = jnp.exp(s_ref[h, n, :, LANES:] - m)
                l_run = l_run + (p0 + p1)
                acc = acc + _dot_nt(jnp.concatenate([p0, p1], axis=1).astype(BF16), vtb_ref[n])
            outs.append(acc / jnp.sum(l_run, axis=1, keepdims=True))
        o_ref[t * blk:(t + 1) * blk, :] = jnp.where(lane < HEAD_DIM, outs[0], outs[1]).astype(o_ref.dtype)


def _attn_prompt(q3, kt_all, vt_all, layer):
    bn, sn, w = q3.shape
    pair_w = 2 * HEAD_DIM
    n_blocks = sn // MOBA_BLOCK
    assert sn % MOBA_BLOCK == 0 and MOBA_TOPK <= n_blocks <= SUBLANES
    kv_spec = pl.BlockSpec((None, None, pair_w, sn), lambda b, p: (layer, b, p, 0))
    q_spec = pl.BlockSpec((None, sn, pair_w), lambda b, p: (b, 0, p))
    return pl.pallas_call(
        functools.partial(_attn_body, n_blocks=n_blocks),
        grid=(bn, w // pair_w),
        in_specs=[q_spec, kv_spec, kv_spec], out_specs=q_spec,
        out_shape=jax.ShapeDtypeStruct((bn, sn, w), BF16),
        scratch_shapes=[pltpu.VMEM((2, n_blocks, pair_w, MOBA_BLOCK), BF16),
                        pltpu.VMEM((n_blocks, pair_w, MOBA_BLOCK), BF16),
                        pltpu.VMEM((2, n_blocks, MOBA_BLOCK, MOBA_BLOCK), F32)],
        compiler_params=_params(("parallel", "parallel")), name="attn_prompt",
    )(q3, kt_all, vt_all)


def _select_body(q_ref, ks_ref, pt_ref, o_ref, *, past, n_blocks, n_pages):
    n_new = q_ref.shape[1]
    lane = lax.broadcasted_iota(jnp.int32, (n_new, LANES), 1)
    q_pos = past + lax.broadcasted_iota(jnp.int32, (n_new, LANES), 0)
    valid = (lane < q_pos // MOBA_BLOCK) & (lane < n_blocks)
    pt_row = jnp.broadcast_to(pt_ref[...], (n_new, n_pages))
    page_lane = lax.broadcasted_iota(jnp.int32, (n_new, n_pages), 1)
    for h in range(N_HEADS):
        km = ks_ref[h] * (1.0 / MOBA_BLOCK)
        g = jnp.dot(q_ref[h], km, preferred_element_type=F32, precision=lax.Precision.HIGHEST)
        g = jnp.where(valid, g, -jnp.inf)
        out = jnp.zeros((n_new, LANES), jnp.int32)
        for j in range(MOBA_TOPK):
            best = jnp.max(g, axis=1, keepdims=True)
            idx = jnp.min(jnp.where(g == best, lane, LANES), axis=1, keepdims=True)
            ok = jnp.sum(jnp.where((lane == idx) & valid, 1, 0), axis=1, keepdims=True)
            out = jnp.where(lane == MOBA_TOPK * PAGES_PER_BLOCK + j, ok, out)
            for pp in range(PAGES_PER_BLOCK):
                page = jnp.minimum(jnp.minimum(idx, n_blocks - 1) * PAGES_PER_BLOCK + pp, n_pages - 1)
                pool = jnp.sum(jnp.where(page_lane == page, pt_row, 0), axis=1, keepdims=True)
                out = jnp.where(lane == j * PAGES_PER_BLOCK + pp, pool, out)
            g = jnp.where(lane == idx, -jnp.inf, g)
        o_ref[h] = out


def _select_sample(q4, ksum, page_table, past, n_blocks):
    bn, hn, qn, dh = q4.shape
    n_pages = page_table.shape[1]
    return pl.pallas_call(
        functools.partial(_select_body, past=past, n_blocks=n_blocks, n_pages=n_pages),
        grid=(bn,),
        in_specs=[pl.BlockSpec((None, hn, qn, dh), lambda b: (b, 0, 0, 0)),
                  pl.BlockSpec((None, hn, dh, LANES), lambda b: (b, 0, 0, 0)),
                  pl.BlockSpec((None, 1, n_pages), lambda b: (b, 0, 0))],
        out_specs=pl.BlockSpec((None, hn, qn, LANES), lambda b: (b, 0, 0, 0)),
        out_shape=jax.ShapeDtypeStruct((bn, hn, qn, LANES), jnp.int32),
        compiler_params=_params(("parallel",)), name="select_sample",
    )(q4, ksum, page_table.reshape(bn, 1, n_pages))


def _attn_sample_body(pages_ref, ok_ref, q_ref, kn_ref, vn_ref, ck_hbm, cv_hbm, o_ref, kbuf, vbuf, sem,
                      *, past, n_new, layer):
    per_q = MOBA_TOPK * PAGES_PER_BLOCK
    n_sel = n_new * per_q
    n_fetch = n_sel + PAGES_PER_BLOCK
    n_heads = pl.num_programs(1)
    step = pl.program_id(0) * n_heads + pl.program_id(1)
    n_steps = pl.num_programs(0) * n_heads
    slot = step % 2

    def copies(st, sl, i):
        page = pages_ref[st * n_fetch + i]
        head = st % n_heads
        return (pltpu.make_async_copy(ck_hbm.at[layer, page, head], kbuf.at[sl, i], sem.at[0, sl]),
                pltpu.make_async_copy(cv_hbm.at[layer, page, head], vbuf.at[sl, i], sem.at[1, sl]))

    def start_all(st, sl):
        for i in range(n_fetch):
            for c in copies(st, sl, i):
                c.start()

    @pl.when(step == 0)
    def _():
        start_all(step, slot)

    @pl.when(step + 1 < n_steps)
    def _():
        start_all(step + 1, 1 - slot)

    for i in range(n_fetch):
        for c in copies(step, slot, i):
            c.wait()

    ok_base = step * n_new * MOBA_TOPK
    scale = HEAD_DIM ** -0.5
    blk_w = PAGES_PER_BLOCK * PAGE_SIZE
    blk_lane = lax.broadcasted_iota(jnp.int32, (n_new, blk_w), 1)
    new_lane = lax.broadcasted_iota(jnp.int32, (n_new, PAGE_SIZE), 1)

    def block(buf, first):
        return jnp.concatenate([buf[slot, first + pp] for pp in range(PAGES_PER_BLOCK)], axis=1).astype(BF16)

    qb = (q_ref[...] * scale).astype(BF16)
    blk_row = lax.broadcasted_iota(jnp.int32, (n_new, blk_w), 0)
    out_row = lax.broadcasted_iota(jnp.int32, (n_new, HEAD_DIM), 0)
    q_pos = past + blk_row
    own_start = (q_pos // MOBA_BLOCK) * MOBA_BLOCK

    scores = []
    for j in range(MOBA_TOPK):
        s_j = jnp.full((n_new, blk_w), -jnp.inf, F32)
        for qi in range(n_new):
            ok = ok_ref[ok_base + qi * MOBA_TOPK + j] > 0
            s = _dot(qb, block(kbuf, qi * per_q + j * PAGES_PER_BLOCK))
            s_j = jnp.where((blk_row == qi) & ok, s, s_j)
        scores.append(s_j)
    pos = past - blk_w + blk_lane
    scores.append(jnp.where((pos >= own_start) & (pos <= q_pos), _dot(qb, block(kbuf, n_sel)), -jnp.inf))
    new_pos = past + new_lane
    s_new = jnp.where((new_lane < n_new) & (new_pos >= own_start[:, :PAGE_SIZE])
                      & (new_pos <= q_pos[:, :PAGE_SIZE]),
                      _dot(qb, kn_ref[...].astype(BF16)), -jnp.inf)

    m_run = scores[0]
    for s in scores[1:]:
        m_run = jnp.maximum(m_run, s)
    m = jnp.maximum(jnp.max(m_run, axis=1, keepdims=True), jnp.max(s_new, axis=1, keepdims=True))
    p_new = jnp.exp(s_new - m)
    probs = [jnp.exp(s - m) for s in scores]
    l_run = probs[0]
    for p in probs[1:]:
        l_run = l_run + p
    l = jnp.sum(l_run, axis=1, keepdims=True) + jnp.sum(p_new, axis=1, keepdims=True)
    acc = (_dot_nt(p_new.astype(BF16), vn_ref[...].astype(BF16))
           + _dot_nt(probs[MOBA_TOPK].astype(BF16), block(vbuf, n_sel)))
    for j in range(MOBA_TOPK):
        pb = probs[j].astype(BF16)
        for qi in range(n_new):
            o = _dot_nt(pb, block(vbuf, qi * per_q + j * PAGES_PER_BLOCK))
            acc = acc + jnp.where(out_row == qi, o, 0.0)
    o_ref[...] = acc / l


def _attn_sample(sel, page_table, q4, kn_t, vn_t, ck_t, cv_t, layer, past):
    bn, hn, qn, dh = q4.shape
    n_pages = page_table.shape[1]
    assert qn == SUBLANES
    pad = ((0, 0), (0, 0), (0, 0), (0, PAGE_SIZE - qn))
    kn_t, vn_t = jnp.pad(kn_t, pad), jnp.pad(vn_t, pad)
    per_q = MOBA_TOPK * PAGES_PER_BLOCK
    n_fetch = qn * per_q + PAGES_PER_BLOCK
    tail_pages = jnp.broadcast_to(page_table[:, None, n_pages - PAGES_PER_BLOCK:], (bn, hn, PAGES_PER_BLOCK))
    pages = jnp.concatenate([sel[..., :per_q].reshape(bn, hn, qn * per_q), tail_pages], axis=-1)
    ok = sel[..., per_q:per_q + MOBA_TOPK]

    bh = lambda shape: pl.BlockSpec((None, None) + shape, lambda b, h, pages_r, ok_r: (b, h, 0, 0))
    any_spec = pl.BlockSpec(memory_space=pl.ANY)
    return pl.pallas_call(
        functools.partial(_attn_sample_body, past=past, n_new=qn, layer=layer),
        grid_spec=pltpu.PrefetchScalarGridSpec(
            num_scalar_prefetch=2, grid=(bn, hn),
            in_specs=[bh((qn, dh)), bh((dh, PAGE_SIZE)), bh((dh, PAGE_SIZE)), any_spec, any_spec],
            out_specs=bh((qn, dh)),
            scratch_shapes=[pltpu.VMEM((2, n_fetch, dh, PAGE_SIZE), F32),
                            pltpu.VMEM((2, n_fetch, dh, PAGE_SIZE), F32),
                            pltpu.SemaphoreType.DMA((2, 2))]),
        out_shape=jax.ShapeDtypeStruct((bn, hn, qn, dh), F32),
        compiler_params=_params(("arbitrary", "arbitrary")), name="attn_sample",
    )(pages.reshape(-1), ok.reshape(-1), q4, kn_t, vn_t, ck_t, cv_t)


FFN_TM = 512
PROJ_TM = 512
SSM_T_PROMPT = 64


def kernel(x_prompt, x_sample, cache_k, cache_v, state_ssm_re, state_ssm_im, page_table, norm_ffn1, ffn1_w_gate, ffn1_w_up, ffn1_w_down, norm_mix, w_in, ssm_log_dt, ssm_a_re, ssm_a_im, ssm_b_re, ssm_b_im, ssm_c_re, ssm_c_im, ssm_d, glu_w, glu_b, w_out, norm_ffn2, ffn2_w_gate, ffn2_w_up, ffn2_w_down, norm_final):
    depth = norm_ffn1.shape[0]
    bp, sp, d = x_prompt.shape
    bs, ss, _ = x_sample.shape
    n_pages = page_table.shape[1]
    past = n_pages * PAGE_SIZE
    n_blocks_past = past // MOBA_BLOCK
    assert n_blocks_past >= MOBA_TOPK

    ck_t = cache_k.transpose(0, 1, 3, 4, 2)
    cv_t = cache_v.transpose(0, 1, 3, 4, 2)

    tok_p, feat_p = _rope_tables(np.arange(sp))
    pos_s = np.tile(past + np.arange(ss), bs)
    tok_s, feat_s = _rope_tables(pos_s)

    xp = x_prompt.reshape(bp * sp, d)
    xs = x_sample.reshape(bs * ss, d)
    zeros_p = jnp.zeros((bp, N_STATE), F32)
    row = lambda v: v.reshape(1, -1)

    wg1, wu1, wd1 = ffn1_w_gate.astype(BF16), ffn1_w_up.astype(BF16), ffn1_w_down.astype(BF16)
    wg2, wu2, wd2 = ffn2_w_gate.astype(BF16), ffn2_w_up.astype(BF16), ffn2_w_down.astype(BF16)
    wuq = w_in[:, :, :SSM_WIDTH + ATTN_WIDTH].astype(BF16)
    wkv_t = w_in[:, :, SSM_WIDTH + ATTN_WIDTH:].transpose(0, 2, 1).astype(BF16)
    wo = w_out.astype(BF16)
    gw = glu_w.astype(BF16)

    kv_p = None
    hr_p, hi_p, k_s, v_s, hr_s, hi_s = [], [], [], [], [], []
    n_tok_s = bs * ss
    for l in range(depth):
        abr, abi, bbr_t, bbi_t = _ssm_prep(ssm_log_dt[l], ssm_a_re[l], ssm_a_im[l], ssm_b_re[l], ssm_b_im[l])
        wbu, cmat, ar8, ai8 = _ssm_layouts(abr, abi, bbr_t, bbi_t, ssm_c_re[l], ssm_c_im[l])
        ssm_consts = (wbu, cmat, ar8, ai8, row(ssm_d[l]), gw, row(glu_b[l]), l)
        last = l == depth - 1

        half_s = bs // 2
        xp, ksum_a = _ffn(xp, row(norm_ffn1[l]), wg1, wu1, wd1, l, tm=FFN_TM,
                          block_sums=(page_table, ck_t, l, 0, half_s))
        u, q, kt_all, vt_all = _proj(xp.reshape(bp, sp, d), row(norm_mix[l]), wuq, wkv_t, l, tok_p, feat_p,
                                     tm=PROJ_TM, layer=l, n_layers=depth, prev=kv_p)
        kv_p = (kt_all, vt_all)
        y_ssm, hr, hi = _ssm(u, zeros_p, zeros_p, *ssm_consts, t_steps=SSM_T_PROMPT)
        hr_p.append(hr)
        hi_p.append(hi)
        y_att = _attn_prompt(q, kt_all, vt_all, l)
        xp, ksum_b = _ffn(xp, row(norm_ffn2[l]), wg2, wu2, wd2, l, tm=FFN_TM,
                          premix=(y_ssm.reshape(bp * sp, SSM_WIDTH), y_att.reshape(bp * sp, ATTN_WIDTH), wo),
                          final_g=row(norm_final) if last else None,
                          block_sums=(page_table, ck_t, l, half_s, bs - half_s))
        ksum = jnp.concatenate([ksum_a, ksum_b], axis=0)

        xs = _ffn(xs, row(norm_ffn1[l]), wg1, wu1, wd1, l, tm=n_tok_s)
        u, q, kt, vt = _proj(xs.reshape(1, n_tok_s, d), row(norm_mix[l]), wuq, wkv_t, l, tok_s, feat_s,
                             tm=n_tok_s, layer=0, n_layers=1)
        y_ssm, hr, hi = _ssm(u.reshape(bs, ss, SSM_WIDTH), state_ssm_re[l].reshape(bs, N_STATE),
                             state_ssm_im[l].reshape(bs, N_STATE), *ssm_consts, t_steps=ss)
        hr_s.append(hr)
        hi_s.append(hi)
        kn_t = kt.reshape(N_HEADS, HEAD_DIM, bs, ss).transpose(2, 0, 1, 3)
        vn_t = vt.reshape(N_HEADS, HEAD_DIM, bs, ss).transpose(2, 0, 1, 3)
        k_s.append(kn_t.transpose(0, 3, 1, 2))
        v_s.append(vn_t.transpose(0, 3, 1, 2))
        q4 = q.reshape(bs, ss, N_HEADS, HEAD_DIM).transpose(0, 2, 1, 3)
        sel = _select_sample(q4, ksum, page_table, past, n_blocks_past)
        att = _attn_sample(sel, page_table, q4, kn_t, vn_t, ck_t, cv_t, l, past)
        y_att = att.transpose(0, 2, 1, 3).reshape(n_tok_s, ATTN_WIDTH).astype(BF16)
        xs = _ffn(xs, row(norm_ffn2[l]), wg2, wu2, wd2, l, tm=n_tok_s,
                  premix=(y_ssm.reshape(n_tok_s, SSM_WIDTH), y_att, wo),
                  final_g=row(norm_final) if last else None)

    kt_all, vt_all = kv_p
    to_tokens = lambda t: t.reshape(depth, bp, N_HEADS, HEAD_DIM, sp).transpose(0, 1, 4, 2, 3)
    state = lambda hs, bn: jnp.stack(hs).reshape(depth, bn, SSM_GROUPS, SSM_STATE)
    return (xp.reshape(bp, sp, d), xs.reshape(bs, ss, d),
            to_tokens(kt_all), to_tokens(vt_all), state(hr_p, bp), state(hi_p, bp),
            jnp.stack(k_s), jnp.stack(v_s), state(hr_s, bs), state(hi_s, bs))
```

```python
import functools
import math

import numpy as np
import jax
import jax.numpy as jnp
from jax import lax
from jax.experimental import pallas as pl
from jax.experimental.pallas import tpu as pltpu

F32 = jnp.float32
BF16 = jnp.bfloat16

D_MODEL = 1024
SSM_WIDTH = 512
SSM_GROUP = 16
SSM_GROUPS = 32
SSM_STATE = 64
ATTN_WIDTH = 512
HEAD_DIM = 64
N_HEADS = 8
ROT_DIM = 16
ROPE_THETA = 500000.0
MOBA_BLOCK = 256
MOBA_TOPK = 3
PAGE_SIZE = 128
PAGES_PER_BLOCK = MOBA_BLOCK // PAGE_SIZE
D_FF = 2816
NORM_EPS = 1e-6

LANES = 128
SUBLANES = 8
MXU_DIM = 256
N_STATE = SSM_GROUPS * SSM_STATE
VMEM_LIMIT = 56 * 1024 * 1024
MASK_BIAS = -(2.0 ** 100)

NT_DIMS = (((1,), (1,)), ((), ()))


def _dot(a, b):
    return jnp.dot(a, b, preferred_element_type=F32)


def _dot_nt(a, b, precision=None):
    return lax.dot_general(a, b, NT_DIMS, preferred_element_type=F32, precision=precision)


def _rms(x, g):
    return x * lax.rsqrt(jnp.mean(x * x, axis=-1, keepdims=True) + NORM_EPS) * g


def _const_spec(shape):
    nd = len(shape)
    return pl.BlockSpec(shape, lambda *_: (0,) * nd, pipeline_mode=pl.Buffered(1))


def _layer_spec(stacked, layer):
    nd = stacked.ndim - 1
    return pl.BlockSpec((None,) + stacked.shape[1:], lambda *_: (layer,) + (0,) * nd,
                        pipeline_mode=pl.Buffered(1))


def _params(sem, vmem=VMEM_LIMIT):
    return pltpu.CompilerParams(dimension_semantics=sem, vmem_limit_bytes=vmem)


def _ff_chunks(d_ff):
    assert d_ff % MXU_DIM == 0
    n = d_ff // MXU_DIM
    sizes, left = [], n
    while left > 0:
        take = min(4, left)
        sizes.append(take * MXU_DIM)
        left -= take
    return sizes


def _block_sums_step(pt_ref, cache_hbm, ks_ref, pbuf, sem, *, layer, row0, n_pages, steps_per_row):
    pps = pbuf.shape[1]
    bps = pps // PAGES_PER_BLOCK
    step, n_steps = pl.program_id(0), pl.num_programs(0)
    slot = step % 2

    def copies(st, sl):
        base = (row0 + st // steps_per_row) * n_pages + (st % steps_per_row) * pps
        return [pltpu.make_async_copy(cache_hbm.at[layer, pt_ref[base + k]], pbuf.at[sl, k], sem.at[sl])
                for k in range(pps)]

    @pl.when(step == 0)
    def _():
        for c in copies(step, slot):
            c.start()

    @pl.when(step + 1 < n_steps)
    def _():
        for c in copies(step + 1, 1 - slot):
            c.start()

    for c in copies(step, slot):
        c.wait()

    part = step % steps_per_row

    @pl.when(part == 0)
    def _():
        ks_ref[...] = jnp.zeros(ks_ref.shape, F32)

    acc = ks_ref[...]
    lane = lax.broadcasted_iota(jnp.int32, acc.shape, 2)
    for j in range(bps):
        blk = pbuf[slot, PAGES_PER_BLOCK * j]
        for pp in range(1, PAGES_PER_BLOCK):
            blk = blk + pbuf[slot, PAGES_PER_BLOCK * j + pp]
        acc = jnp.where(lane == part * bps + j, jnp.sum(blk, axis=-1, keepdims=True), acc)
    ks_ref[...] = acc


def _ffn_body(*refs, premix, final, chunks, side):
    it = iter(refs)
    pt_ref = next(it) if side else None
    x_ref = next(it)
    if premix:
        ys_ref, ya_ref, wo_ref = next(it), next(it), next(it)
    g_ref, wg_ref, wu_ref, wd_ref = next(it), next(it), next(it), next(it)
    gf_ref = next(it) if final else None
    cache_hbm = next(it) if side else None
    o_ref = next(it)
    if side:
        ks_ref, pbuf, sem = next(it), next(it), next(it)
        _block_sums_step(pt_ref, cache_hbm, ks_ref, pbuf, sem, **side)

    x = x_ref[...]
    if premix:
        half = ys_ref.shape[1]
        x = x + _dot(ys_ref[...], wo_ref[:half, :]) + _dot(ya_ref[...], wo_ref[half:, :])
    xn = _rms(x, g_ref[...]).astype(BF16)
    acc = jnp.zeros(x.shape, F32)
    off = 0
    for sz in chunks:
        g = _dot(xn, wg_ref[:, off:off + sz])
        u = _dot(xn, wu_ref[:, off:off + sz])
        h = (jax.nn.silu(g) * u).astype(BF16)
        acc = acc + _dot(h, wd_ref[off:off + sz, :])
        off += sz
    y = x + 0.5 * acc
    if final:
        y = _rms(y, gf_ref[...])
    o_ref[...] = y


def _ffn(x, g, wg, wu, wd, layer, *, premix=None, final_g=None, tm, block_sums=None):
    m, d = x.shape
    d_ff = wg.shape[-1]
    assert m % tm == 0
    n_steps = m // tm
    row = lambda i, *_: (i, 0)
    args, specs = [x], [pl.BlockSpec((tm, d), row)]
    if premix is not None:
        ys, ya, wo = premix
        args += [ys, ya, wo]
        specs += [pl.BlockSpec((tm, ys.shape[1]), row), pl.BlockSpec((tm, ya.shape[1]), row),
                  _layer_spec(wo, layer)]
    args += [g, wg, wu, wd]
    specs += [_const_spec(g.shape), _layer_spec(wg, layer), _layer_spec(wu, layer), _layer_spec(wd, layer)]
    if final_g is not None:
        args.append(final_g)
        specs.append(_const_spec(final_g.shape))
    out_specs = pl.BlockSpec((tm, d), row)
    out_shape = jax.ShapeDtypeStruct((m, d), F32)
    side, scratch, prefetch = None, [], []
    if block_sums is not None:
        page_table, cache_t, layer, row0, n_rows = block_sums
        n_pages = page_table.shape[1]
        assert n_steps % n_rows == 0
        steps_per_row = n_steps // n_rows
        assert n_pages % (steps_per_row * PAGES_PER_BLOCK) == 0 and n_pages // PAGES_PER_BLOCK <= LANES
        pps = n_pages // steps_per_row
        page_shape = cache_t.shape[2:]
        side = dict(layer=layer, row0=row0, n_pages=n_pages, steps_per_row=steps_per_row)
        prefetch = [page_table.reshape(-1)]
        args.append(cache_t)
        specs.append(pl.BlockSpec(memory_space=pl.ANY))
        ks_block = page_shape[:2] + (LANES,)
        out_specs = [out_specs, pl.BlockSpec((None,) + ks_block, lambda i, *_: (i // steps_per_row, 0, 0, 0))]
        out_shape = [out_shape, jax.ShapeDtypeStruct((n_rows,) + ks_block, F32)]
        scratch = [pltpu.VMEM((2, pps) + page_shape, F32), pltpu.SemaphoreType.DMA((2,))]
    body = functools.partial(_ffn_body, premix=premix is not None, final=final_g is not None,
                             chunks=_ff_chunks(d_ff), side=side)
    return pl.pallas_call(
        body,
        grid_spec=pltpu.PrefetchScalarGridSpec(
            num_scalar_prefetch=len(prefetch), grid=(n_steps,), in_specs=specs, out_specs=out_specs,
            scratch_shapes=scratch),
        out_shape=out_shape,
        compiler_params=_params(("arbitrary",) if side else ("parallel",)), name="ffn")(*prefetch, *args)


def _rope_tables(positions):
    half = ROT_DIM // 2
    pos = np.asarray(positions, np.float64)
    inv = np.power(ROPE_THETA, -np.arange(half, dtype=np.float64) * 2.0 / ROT_DIM)
    ang = pos[:, None] * inv[None, :]
    cos, sin = np.cos(ang), np.sin(ang)
    d = np.arange(LANES) % HEAD_DIM
    f = d % half
    c = np.where(d[None, :] < ROT_DIM, cos[:, f], 1.0)
    s_next = np.where(d[None, :] < half, -sin[:, f], 0.0)
    s_prev = np.where((d[None, :] >= half) & (d[None, :] < ROT_DIM), sin[:, f], 0.0)
    tok = tuple(jnp.asarray(t, F32) for t in (c, s_next, s_prev))
    feat = (jnp.asarray(cos.T, F32), jnp.asarray(sin.T, F32))
    return tok, feat


def _proj_body(x_ref, g_ref, wuq_ref, wkv_ref, c_ref, sn_ref, sp_ref, ct_ref, st_ref, *rest, layer):
    u_ref, q_ref, kt_ref, vt_ref = rest[-4:]
    if len(kt_ref.shape) == 3:
        for l in range(kt_ref.shape[0]):
            if l != layer:
                kt_ref[l] = jnp.zeros(kt_ref.shape[1:], F32)
                vt_ref[l] = jnp.zeros(vt_ref.shape[1:], F32)
        kt_ref, vt_ref = kt_ref.at[layer], vt_ref.at[layer]
    half = ROT_DIM // 2
    hb = _rms(x_ref[...], g_ref[...]).astype(BF16)
    uq = _dot(hb, wuq_ref[...])
    u_ref[...] = uq[:, :SSM_WIDTH]
    c, sn, sp = c_ref[...], sn_ref[...], sp_ref[...]
    for j in range(ATTN_WIDTH // LANES):
        qc = uq[:, SSM_WIDTH + j * LANES:SSM_WIDTH + (j + 1) * LANES]
        q_ref[:, j * LANES:(j + 1) * LANES] = (
            qc * c + pltpu.roll(qc, LANES - half, axis=1) * sn + pltpu.roll(qc, half, axis=1) * sp)
    kv = _dot_nt(wkv_ref[...], hb)
    ct, st = ct_ref[...], st_ref[...]
    for h in range(N_HEADS):
        r = h * HEAD_DIM
        a, b = kv[r:r + half], kv[r + half:r + ROT_DIM]
        kt_ref[r:r + half, :] = a * ct - b * st
        kt_ref[r + half:r + ROT_DIM, :] = b * ct + a * st
        kt_ref[r + ROT_DIM:r + HEAD_DIM, :] = kv[r + ROT_DIM:r + HEAD_DIM]
    vt_ref[...] = kv[ATTN_WIDTH:]


def _proj(x3, g, wuq, wkv_t, w_layer, tok_tabs, feat_tabs, *, tm, layer, n_layers, prev=None):
    bn, sn, d = x3.shape
    assert sn % tm == 0
    tok_spec = pl.BlockSpec((tm, LANES), lambda b, s: (s, 0))
    feat_spec = pl.BlockSpec((ROT_DIM // 2, tm), lambda b, s: (0, s))
    in_specs = [pl.BlockSpec((None, tm, d), lambda b, s: (b, s, 0)), _const_spec(g.shape),
                _layer_spec(wuq, w_layer), _layer_spec(wkv_t, w_layer),
                tok_spec, tok_spec, tok_spec, feat_spec, feat_spec]
    args = [x3, g, wuq, wkv_t, *tok_tabs, *feat_tabs]
    aliases = {}
    if prev is not None:
        in_specs += [pl.BlockSpec(memory_space=pl.ANY)] * 2
        aliases = {len(args): 2, len(args) + 1: 3}
        args += list(prev)
    tok_out = pl.BlockSpec((None, tm, SSM_WIDTH), lambda b, s: (b, s, 0))
    if prev is None and n_layers > 1:
        feat_out = pl.BlockSpec((n_layers, None, ATTN_WIDTH, tm), lambda b, s: (0, b, 0, s))
    else:
        feat_out = pl.BlockSpec((None, None, ATTN_WIDTH, tm), lambda b, s: (layer, b, 0, s))
    stacked = jax.ShapeDtypeStruct((n_layers, bn, ATTN_WIDTH, sn), F32)
    return pl.pallas_call(
        functools.partial(_proj_body, layer=layer), grid=(bn, sn // tm), in_specs=in_specs,
        out_specs=[tok_out, tok_out, feat_out, feat_out],
        out_shape=[jax.ShapeDtypeStruct((bn, sn, SSM_WIDTH), F32),
                   jax.ShapeDtypeStruct((bn, sn, ATTN_WIDTH), F32), stacked, stacked],
        input_output_aliases=aliases, compiler_params=_params(("parallel", "parallel")),
        name="proj")(*args)


def _ssm_prep_body(ldt_ref, ar_ref, ai_ref, btr_ref, bti_ref, abr_ref, abi_ref, bbr_ref, bbi_ref):
    dt = jnp.exp(ldt_ref[...])
    ar, ai = ar_ref[...], ai_ref[...]
    mag = jnp.exp(ar * dt)
    abr = mag * jnp.cos(ai * dt)
    abi = mag * jnp.sin(ai * dt)
    abr_ref[...] = abr
    abi_ref[...] = abi
    den = ar * ar + ai * ai
    n_re, n_im = abr - 1.0, abi
    w_re = ((n_re * ar + n_im * ai) / den)[:, None, :]
    w_im = ((n_im * ar - n_re * ai) / den)[:, None, :]
    btr, bti = btr_ref[...], bti_ref[...]
    bbr_ref[...] = w_re * btr - w_im * bti
    bbi_ref[...] = w_re * bti + w_im * btr


def _ssm_prep(log_dt, a_re, a_im, b_re, b_im):
    g, p, h = b_re.shape
    shp = lambda *s: jax.ShapeDtypeStruct(s, F32)
    return pl.pallas_call(
        _ssm_prep_body, out_shape=[shp(g, p), shp(g, p), shp(g, h, p), shp(g, h, p)], name="ssm_prep",
    )(log_dt.reshape(g, 1), a_re, a_im, b_re.transpose(0, 2, 1), b_im.transpose(0, 2, 1))


def _ssm_layouts(abr, abi, bbr_t, bbi_t, c_re, c_im):
    g, h, p = bbr_t.shape
    n_tiles = g // 2
    blk = jnp.concatenate([bbr_t.reshape(n_tiles, 2, h, p), bbi_t.reshape(n_tiles, 2, h, p)], axis=1)
    grp_per_lane_tile = LANES // h
    place = np.zeros((n_tiles, 4, grp_per_lane_tile), np.float32)
    for j in range(n_tiles):
        for k in range(4):
            place[j, k, (2 * j + k % 2) % grp_per_lane_tile] = 1.0
    wbu = blk[:, :, None, :, :] * jnp.asarray(place)[:, :, :, None, None]
    wbu = wbu.transpose(0, 2, 3, 1, 4).reshape(n_tiles, LANES, 4 * p).astype(BF16)
    c_rows = jnp.concatenate([c_re.transpose(0, 2, 1).reshape(n_tiles, 2, p, h),
                              -c_im.transpose(0, 2, 1).reshape(n_tiles, 2, p, h)], axis=1)
    tiles_per_half = n_tiles // 2
    diag = np.zeros((tiles_per_half, 4, tiles_per_half, 2), np.float32)
    for t in range(tiles_per_half):
        for k in range(4):
            diag[t, k, t, k % 2] = 1.0
    cmat = (c_rows.reshape(2, tiles_per_half, 4, p, 1, 1, h)
            * jnp.asarray(diag)[None, :, :, None, :, :, None])
    cmat = cmat.reshape(2, tiles_per_half * 4 * p, tiles_per_half * 2 * h).astype(BF16)
    ar8 = jnp.broadcast_to(abr.reshape(1, g * p), (SUBLANES, g * p))
    ai8 = jnp.broadcast_to(abi.reshape(1, g * p), (SUBLANES, g * p))
    return wbu, cmat, ar8, ai8


def _perm_matrices(t):
    r = SUBLANES * t
    p = np.zeros((r, r), np.float32)
    for b in range(SUBLANES):
        for s in range(t):
            p[s * SUBLANES + b, b * t + s] = 1.0
    return jnp.asarray(p, BF16), jnp.asarray(p.T, BF16)


def _ssm_body(u_ref, h0r_ref, h0i_ref, p_ref, pt_ref, wbu_ref, cm_ref, ar_ref, ai_ref, d_ref, gw_ref,
              gb_ref, y_ref, hr_ref, hi_ref, bu_ref, *, t_steps, n_groups):
    r = SUBLANES * t_steps
    n_tiles = wbu_ref.shape[0]
    tile_w = 2 * LANES
    tiles_per_lane_tile = (LANES // SSM_GROUP) // 2
    rows = lambda g: slice(g * SUBLANES, (g + 1) * SUBLANES)

    @pl.when(pl.program_id(1) == 0)
    def _():
        hr_ref[...] = h0r_ref[...]
        hi_ref[...] = h0i_ref[...]

    perm = p_ref[...]
    u_tb = []
    for g in range(n_groups):
        u = u_ref[rows(g)].reshape(r, SSM_WIDTH)
        u_hi, u_lo = _split_bf16(u)
        uh = _dot(perm, u_hi)
        u_tb.append(uh + _dot(perm, u_lo))
        ub = uh.astype(BF16)
        for j in range(n_tiles):
            lt = j // tiles_per_lane_tile
            bu_ref[g, :, j * tile_w:(j + 1) * tile_w] = _dot(ub[:, lt * LANES:(lt + 1) * LANES], wbu_ref[j])

    for g in range(n_groups):
        hr = [hr_ref[rows(g), j * LANES:(j + 1) * LANES] for j in range(n_tiles)]
        hi = [hi_ref[rows(g), j * LANES:(j + 1) * LANES] for j in range(n_tiles)]
        for t in range(t_steps):
            tr = slice(t * SUBLANES, (t + 1) * SUBLANES)
            for j in range(n_tiles):
                cr = slice(j * tile_w, j * tile_w + LANES)
                ci = slice(j * tile_w + LANES, (j + 1) * tile_w)
                cs = slice(j * LANES, (j + 1) * LANES)
                ar, ai = ar_ref[:, cs], ai_ref[:, cs]
                nr = ar * hr[j] - ai * hi[j] + bu_ref[g, tr, cr]
                ni = ar * hi[j] + ai * hr[j] + bu_ref[g, tr, ci]
                bu_ref[g, tr, cr] = nr
                bu_ref[g, tr, ci] = ni
                hr[j], hi[j] = nr, ni
        hr_ref[rows(g), :] = jnp.concatenate(hr, axis=1)
        hi_ref[rows(g), :] = jnp.concatenate(hi, axis=1)

    half_cols = bu_ref.shape[2] // 2
    for g in range(n_groups):
        y = jnp.concatenate(
            [_dot(bu_ref[g, :, :half_cols].astype(BF16), cm_ref[0]),
             _dot(bu_ref[g, :, half_cols:].astype(BF16), cm_ref[1])], axis=1)
        y = y + d_ref[...] * u_tb[g]
        z = jax.nn.gelu(y)
        gate = jax.nn.sigmoid(_dot(z.astype(BF16), gw_ref[...]) + gb_ref[...])
        o = (z * gate).astype(BF16)
        y_ref[rows(g)] = _dot(pt_ref[...], o).astype(BF16).reshape((SUBLANES,) + y_ref.shape[1:])


SSM_GROUPS_PER_STEP = 2


def _ssm(u3, h0r, h0i, wbu, cmat, ar8, ai8, d, gw, gb, layer, *, t_steps):
    bn, ln, w = u3.shape
    n_groups = SSM_GROUPS_PER_STEP
    nb = n_groups * SUBLANES
    assert bn % nb == 0 and ln % t_steps == 0
    perm, perm_t = _perm_matrices(t_steps)
    r = SUBLANES * t_steps
    state_spec = pl.BlockSpec((nb, N_STATE), lambda b, c: (b, 0))
    seq_spec = pl.BlockSpec((nb, t_steps, w), lambda b, c: (b, c, 0))
    consts = [perm, perm_t, wbu, cmat, ar8, ai8, d, gw, gb]
    return pl.pallas_call(
        functools.partial(_ssm_body, t_steps=t_steps, n_groups=n_groups),
        grid=(bn // nb, ln // t_steps),
        in_specs=[seq_spec, state_spec, state_spec]
        + [_layer_spec(c, layer) if c is gw else _const_spec(c.shape) for c in consts],
        out_specs=[seq_spec, state_spec, state_spec],
        out_shape=[jax.ShapeDtypeStruct((bn, ln, w), BF16),
                   jax.ShapeDtypeStruct((bn, N_STATE), F32), jax.ShapeDtypeStruct((bn, N_STATE), F32)],
        scratch_shapes=[pltpu.VMEM((n_groups, r, 2 * N_STATE), F32)],
        compiler_params=_params(("parallel", "arbitrary")), name="ssm",
    )(u3, h0r, h0i, *consts)


def _select_rows(g, valid, n_idx, k):
    g = jnp.where(valid, g, -jnp.inf)
    cnt = jnp.zeros(g.shape, jnp.int32)
    for m in range(g.shape[0]):
        row = g[m:m + 1, :]
        beats = (row > g) | ((row == g) & (m < n_idx))
        cnt = cnt + beats.astype(jnp.int32)
    return (cnt < k) & valid


def _split_bf16(x):
    hi = x.astype(BF16)
    return hi, (x - hi.astype(F32)).astype(BF16)


def _attn_body(q_ref, kt_ref, vt_ref, o_ref, kaug_ref, vtb_ref, s_ref, *, n_blocks):
    pair_w = 2 * HEAD_DIM
    blk = MOBA_BLOCK
    lane = lax.broadcasted_iota(jnp.int32, (blk, pair_w), 1)

    sums = [jnp.sum(kt_ref[:, n * blk:(n + 1) * blk], axis=1, keepdims=True) for n in range(n_blocks)]
    km_t = jnp.concatenate(sums + [jnp.zeros((pair_w, pair_w - n_blocks), F32)], axis=1)
    km = (km_t * (1.0 / blk)).T[:SUBLANES]
    l8 = lax.broadcasted_iota(jnp.int32, km.shape, 1)
    km_split = [_split_bf16(jnp.where(l8 < HEAD_DIM, km, 0.0)), _split_bf16(jnp.where(l8 >= HEAD_DIM, km, 0.0))]
    rows = lax.broadcasted_iota(jnp.int32, (HEAD_DIM, blk), 0)
    for n in range(n_blocks):
        kb = kt_ref[:, n * blk:(n + 1) * blk].astype(BF16)
        ind = (rows == n).astype(BF16)
        kaug_ref[0, n] = jnp.concatenate([kb[:HEAD_DIM], ind], axis=0)
        kaug_ref[1, n] = jnp.concatenate([ind, kb[HEAD_DIM:]], axis=0)
        vtb_ref[n] = vt_ref[:, n * blk:(n + 1) * blk].astype(BF16)

    n_idx = lax.broadcasted_iota(jnp.int32, (SUBLANES, blk), 0)
    eye = (lax.broadcasted_iota(jnp.int32, (blk, blk), 0)
           == lax.broadcasted_iota(jnp.int32, (blk, blk), 1)).astype(BF16)
    causal = (lax.broadcasted_iota(jnp.int32, (blk, blk), 1)
              <= lax.broadcasted_iota(jnp.int32, (blk, blk), 0))
    pad = jnp.zeros((HEAD_DIM - SUBLANES, blk), F32)
    scale = HEAD_DIM ** -0.5

    for t in range(n_blocks):
        q = q_ref[t * blk:(t + 1) * blk, :]
        qs = q * scale
        if t > MOBA_TOPK:
            q_hi, q_lo = _split_bf16(q)
            bias_rows = []
            for h in range(2):
                k_hi, k_lo = km_split[h]
                g_t = _dot_nt(k_hi, q_hi) + _dot_nt(k_hi, q_lo) + _dot_nt(k_lo, q_hi)
                sel = _select_rows(g_t, n_idx < t, n_idx, MOBA_TOPK) | (n_idx == t)
                bias_rows.append(jnp.where(sel, 0.0, MASK_BIAS))
            b_mat = jnp.concatenate([bias_rows[1], pad, bias_rows[0], pad], axis=0).astype(BF16)
            bias_q = _dot_nt(eye, b_mat)
        else:
            bias_q = jnp.zeros((blk, pair_w), F32)
        outs = []
        for h in range(2):
            mine = (lane < HEAD_DIM) if h == 0 else (lane >= HEAD_DIM)
            qa = jnp.where(mine, qs, bias_q).astype(BF16)
            m_run = None
            for n in range(t + 1):
                s = _dot(qa, kaug_ref[h, n])
                if n == t:
                    s = jnp.where(causal, s, -jnp.inf)
                s_ref[h, n] = s
                mx = jnp.maximum(s[:, :LANES], s[:, LANES:])
                m_run = mx if m_run is None else jnp.maximum(m_run, mx)
            m = jnp.broadcast_to(jnp.max(m_run, axis=1, keepdims=True), (blk, LANES))
            l_run = jnp.zeros((blk, LANES), F32)
            acc = jnp.zeros((blk, pair_w), F32)
            for n in range(t + 1):
                p0 = jnp.exp(s_ref[h, n, :, :LANES] - m)
                p1 = jnp.exp(s_ref[h, n, :, LANES:] - m)
                l_run = l_run + (p0 + p1)
                acc = acc + _dot_nt(jnp.concatenate([p0, p1], axis=1).astype(BF16), vtb_ref[n])
            outs.append(acc / jnp.sum(l_run, axis=1, keepdims=True))
        o_ref[t * blk:(t + 1) * blk, :] = jnp.where(lane < HEAD_DIM, outs[0], outs[1]).astype(o_ref.dtype)


def _attn_prompt(q3, kt_all, vt_all, layer):
    bn, sn, w = q3.shape
    pair_w = 2 * HEAD_DIM
    n_blocks = sn // MOBA_BLOCK
    assert sn % MOBA_BLOCK == 0 and MOBA_TOPK <= n_blocks <= SUBLANES
    kv_spec = pl.BlockSpec((None, None, pair_w, sn), lambda b, p: (layer, b, p, 0))
    q_spec = pl.BlockSpec((None, sn, pair_w), lambda b, p: (b, 0, p))
    return pl.pallas_call(
        functools.partial(_attn_body, n_blocks=n_blocks),
        grid=(bn, w // pair_w),
        in_specs=[q_spec, kv_spec, kv_spec], out_specs=q_spec,
        out_shape=jax.ShapeDtypeStruct((bn, sn, w), BF16),
        scratch_shapes=[pltpu.VMEM((2, n_blocks, pair_w, MOBA_BLOCK), BF16),
                        pltpu.VMEM((n_blocks, pair_w, MOBA_BLOCK), BF16),
                        pltpu.VMEM((2, n_blocks, MOBA_BLOCK, MOBA_BLOCK), F32)],
        compiler_params=_params(("parallel", "parallel")), name="attn_prompt",
    )(q3, kt_all, vt_all)


def _select_body(q_ref, ks_ref, pt_ref, o_ref, *, past, n_blocks, n_pages):
    n_new = q_ref.shape[1]
    lane = lax.broadcasted_iota(jnp.int32, (n_new, LANES), 1)
    q_pos = past + lax.broadcasted_iota(jnp.int32, (n_new, LANES), 0)
    valid = (lane < q_pos // MOBA_BLOCK) & (lane < n_blocks)
    pt_row = jnp.broadcast_to(pt_ref[...], (n_new, n_pages))
    page_lane = lax.broadcasted_iota(jnp.int32, (n_new, n_pages), 1)
    for h in range(N_HEADS):
        km = ks_ref[h] * (1.0 / MOBA_BLOCK)
        g = jnp.dot(q_ref[h], km, preferred_element_type=F32, precision=lax.Precision.HIGHEST)
        g = jnp.where(valid, g, -jnp.inf)
        out = jnp.zeros((n_new, LANES), jnp.int32)
        for j in range(MOBA_TOPK):
            best = jnp.max(g, axis=1, keepdims=True)
            idx = jnp.min(jnp.where(g == best, lane, LANES), axis=1, keepdims=True)
            ok = jnp.sum(jnp.where((lane == idx) & valid, 1, 0), axis=1, keepdims=True)
            out = jnp.where(lane == MOBA_TOPK * PAGES_PER_BLOCK + j, ok, out)
            for pp in range(PAGES_PER_BLOCK):
                page = jnp.minimum(jnp.minimum(idx, n_blocks - 1) * PAGES_PER_BLOCK + pp, n_pages - 1)
                pool = jnp.sum(jnp.where(page_lane == page, pt_row, 0), axis=1, keepdims=True)
                out = jnp.where(lane == j * PAGES_PER_BLOCK + pp, pool, out)
            g = jnp.where(lane == idx, -jnp.inf, g)
        o_ref[h] = out


def _select_sample(q4, ksum, page_table, past, n_blocks):
    bn, hn, qn, dh = q4.shape
    n_pages = page_table.shape[1]
    return pl.pallas_call(
        functools.partial(_select_body, past=past, n_blocks=n_blocks, n_pages=n_pages),
        grid=(bn,),
        in_specs=[pl.BlockSpec((None, hn, qn, dh), lambda b: (b, 0, 0, 0)),
                  pl.BlockSpec((None, hn, dh, LANES), lambda b: (b, 0, 0, 0)),
                  pl.BlockSpec((None, 1, n_pages), lambda b: (b, 0, 0))],
        out_specs=pl.BlockSpec((None, hn, qn, LANES), lambda b: (b, 0, 0, 0)),
        out_shape=jax.ShapeDtypeStruct((bn, hn, qn, LANES), jnp.int32),
        compiler_params=_params(("parallel",)), name="select_sample",
    )(q4, ksum, page_table.reshape(bn, 1, n_pages))


def _attn_sample_body(pages_ref, ok_ref, q_ref, kn_ref, vn_ref, ck_hbm, cv_hbm, o_ref, kbuf, vbuf, sem,
                      *, past, n_new, layer):
    per_q = MOBA_TOPK * PAGES_PER_BLOCK
    n_sel = n_new * per_q
    n_fetch = n_sel + PAGES_PER_BLOCK
    n_heads = pl.num_programs(1)
    step = pl.program_id(0) * n_heads + pl.program_id(1)
    n_steps = pl.num_programs(0) * n_heads
    slot = step % 2

    def copies(st, sl, i):
        page = pages_ref[st * n_fetch + i]
        head = st % n_heads
        return (pltpu.make_async_copy(ck_hbm.at[layer, page, head], kbuf.at[sl, i], sem.at[0, sl]),
                pltpu.make_async_copy(cv_hbm.at[layer, page, head], vbuf.at[sl, i], sem.at[1, sl]))

    def start_all(st, sl):
        for i in range(n_fetch):
            for c in copies(st, sl, i):
                c.start()

    @pl.when(step == 0)
    def _():
        start_all(step, slot)

    @pl.when(step + 1 < n_steps)
    def _():
        start_all(step + 1, 1 - slot)

    for i in range(n_fetch):
        for c in copies(step, slot, i):
            c.wait()

    ok_base = step * n_new * MOBA_TOPK
    scale = HEAD_DIM ** -0.5
    blk_w = PAGES_PER_BLOCK * PAGE_SIZE
    blk_lane = lax.broadcasted_iota(jnp.int32, (n_new, blk_w), 1)
    new_lane = lax.broadcasted_iota(jnp.int32, (n_new, PAGE_SIZE), 1)

    def block(buf, first):
        return jnp.concatenate([buf[slot, first + pp] for pp in range(PAGES_PER_BLOCK)], axis=1).astype(BF16)

    qb = (q_ref[...] * scale).astype(BF16)
    blk_row = lax.broadcasted_iota(jnp.int32, (n_new, blk_w), 0)
    out_row = lax.broadcasted_iota(jnp.int32, (n_new, HEAD_DIM), 0)
    q_pos = past + blk_row
    own_start = (q_pos // MOBA_BLOCK) * MOBA_BLOCK

    scores = []
    for j in range(MOBA_TOPK):
        s_j = jnp.full((n_new, blk_w), -jnp.inf, F32)
        for qi in range(n_new):
            ok = ok_ref[ok_base + qi * MOBA_TOPK + j] > 0
            s = _dot(qb, block(kbuf, qi * per_q + j * PAGES_PER_BLOCK))
            s_j = jnp.where((blk_row == qi) & ok, s, s_j)
        scores.append(s_j)
    pos = past - blk_w + blk_lane
    scores.append(jnp.where((pos >= own_start) & (pos <= q_pos), _dot(qb, block(kbuf, n_sel)), -jnp.inf))
    new_pos = past + new_lane
    s_new = jnp.where((new_lane < n_new) & (new_pos >= own_start[:, :PAGE_SIZE])
                      & (new_pos <= q_pos[:, :PAGE_SIZE]),
                      _dot(qb, kn_ref[...].astype(BF16)), -jnp.inf)

    m_run = scores[0]
    for s in scores[1:]:
        m_run = jnp.maximum(m_run, s)
    m = jnp.maximum(jnp.max(m_run, axis=1, keepdims=True), jnp.max(s_new, axis=1, keepdims=True))
    p_new = jnp.exp(s_new - m)
    probs = [jnp.exp(s - m) for s in scores]
    l_run = probs[0]
    for p in probs[1:]:
        l_run = l_run + p
    l = jnp.sum(l_run, axis=1, keepdims=True) + jnp.sum(p_new, axis=1, keepdims=True)
    acc = (_dot_nt(p_new.astype(BF16), vn_ref[...].astype(BF16))
           + _dot_nt(probs[MOBA_TOPK].astype(BF16), block(vbuf, n_sel)))
    for j in range(MOBA_TOPK):
        pb = probs[j].astype(BF16)
        for qi in range(n_new):
            o = _dot_nt(pb, block(vbuf, qi * per_q + j * PAGES_PER_BLOCK))
            acc = acc + jnp.where(out_row == qi, o, 0.0)
    o_ref[...] = acc / l


def _attn_sample(sel, page_table, q4, kn_t, vn_t, ck_t, cv_t, layer, past):
    bn, hn, qn, dh = q4.shape
    n_pages = page_table.shape[1]
    assert qn == SUBLANES
    pad = ((0, 0), (0, 0), (0, 0), (0, PAGE_SIZE - qn))
    kn_t, vn_t = jnp.pad(kn_t, pad), jnp.pad(vn_t, pad)
    per_q = MOBA_TOPK * PAGES_PER_BLOCK
    n_fetch = qn * per_q + PAGES_PER_BLOCK
    tail_pages = jnp.broadcast_to(page_table[:, None, n_pages - PAGES_PER_BLOCK:], (bn, hn, PAGES_PER_BLOCK))
    pages = jnp.concatenate([sel[..., :per_q].reshape(bn, hn, qn * per_q), tail_pages], axis=-1)
    ok = sel[..., per_q:per_q + MOBA_TOPK]

    bh = lambda shape: pl.BlockSpec((None, None) + shape, lambda b, h, pages_r, ok_r: (b, h, 0, 0))
    any_spec = pl.BlockSpec(memory_space=pl.ANY)
    return pl.pallas_call(
        functools.partial(_attn_sample_body, past=past, n_new=qn, layer=layer),
        grid_spec=pltpu.PrefetchScalarGridSpec(
            num_scalar_prefetch=2, grid=(bn, hn),
            in_specs=[bh((qn, dh)), bh((dh, PAGE_SIZE)), bh((dh, PAGE_SIZE)), any_spec, any_spec],
            out_specs=bh((qn, dh)),
            scratch_shapes=[pltpu.VMEM((2, n_fetch, dh, PAGE_SIZE), F32),
                            pltpu.VMEM((2, n_fetch, dh, PAGE_SIZE), F32),
                            pltpu.SemaphoreType.DMA((2, 2))]),
        out_shape=jax.ShapeDtypeStruct((bn, hn, qn, dh), F32),
        compiler_params=_params(("arbitrary", "arbitrary")), name="attn_sample",
    )(pages.reshape(-1), ok.reshape(-1), q4, kn_t, vn_t, ck_t, cv_t)


FFN_TM = 512
PROJ_TM = 512
SSM_T_PROMPT = 64


def kernel(x_prompt, x_sample, cache_k, cache_v, state_ssm_re, state_ssm_im, page_table, norm_ffn1, ffn1_w_gate, ffn1_w_up, ffn1_w_down, norm_mix, w_in, ssm_log_dt, ssm_a_re, ssm_a_im, ssm_b_re, ssm_b_im, ssm_c_re, ssm_c_im, ssm_d, glu_w, glu_b, w_out, norm_ffn2, ffn2_w_gate, ffn2_w_up, ffn2_w_down, norm_final):
    depth = norm_ffn1.shape[0]
    bp, sp, d = x_prompt.shape
    bs, ss, _ = x_sample.shape
    n_pages = page_table.shape[1]
    past = n_pages * PAGE_SIZE
    n_blocks_past = past // MOBA_BLOCK
    assert n_blocks_past >= MOBA_TOPK

    ck_t = cache_k.transpose(0, 1, 3, 4, 2)
    cv_t = cache_v.transpose(0, 1, 3, 4, 2)

    tok_p, feat_p = _rope_tables(np.arange(sp))
    pos_s = np.tile(past + np.arange(ss), bs)
    tok_s, feat_s = _rope_tables(pos_s)

    xp = x_prompt.reshape(bp * sp, d)
    xs = x_sample.reshape(bs * ss, d)
    zeros_p = jnp.zeros((bp, N_STATE), F32)
    row = lambda v: v.reshape(1, -1)

    wg1, wu1, wd1 = ffn1_w_gate.astype(BF16), ffn1_w_up.astype(BF16), ffn1_w_down.astype(BF16)
    wg2, wu2, wd2 = ffn2_w_gate.astype(BF16), ffn2_w_up.astype(BF16), ffn2_w_down.astype(BF16)
    wuq = w_in[:, :, :SSM_WIDTH + ATTN_WIDTH].astype(BF16)
    wkv_t = w_in[:, :, SSM_WIDTH + ATTN_WIDTH:].transpose(0, 2, 1).astype(BF16)
    wo = w_out.astype(BF16)
    gw = glu_w.astype(BF16)

    kv_p = None
    hr_p, hi_p, k_s, v_s, hr_s, hi_s = [], [], [], [], [], []
    n_tok_s = bs * ss
    for l in range(depth):
        abr, abi, bbr_t, bbi_t = _ssm_prep(ssm_log_dt[l], ssm_a_re[l], ssm_a_im[l], ssm_b_re[l], ssm_b_im[l])
        wbu, cmat, ar8, ai8 = _ssm_layouts(abr, abi, bbr_t, bbi_t, ssm_c_re[l], ssm_c_im[l])
        ssm_consts = (wbu, cmat, ar8, ai8, row(ssm_d[l]), gw, row(glu_b[l]), l)
        last = l == depth - 1

        half_s = bs // 2
        xp, ksum_a = _ffn(xp, row(norm_ffn1[l]), wg1, wu1, wd1, l, tm=FFN_TM,
                          block_sums=(page_table, ck_t, l, 0, half_s))
        u, q, kt_all, vt_all = _proj(xp.reshape(bp, sp, d), row(norm_mix[l]), wuq, wkv_t, l, tok_p, feat_p,
                                     tm=PROJ_TM, layer=l, n_layers=depth, prev=kv_p)
        kv_p = (kt_all, vt_all)
        y_ssm, hr, hi = _ssm(u, zeros_p, zeros_p, *ssm_consts, t_steps=SSM_T_PROMPT)
        hr_p.append(hr)
        hi_p.append(hi)
        y_att = _attn_prompt(q, kt_all, vt_all, l)
        xp, ksum_b = _ffn(xp, row(norm_ffn2[l]), wg2, wu2, wd2, l, tm=FFN_TM,
                          premix=(y_ssm.reshape(bp * sp, SSM_WIDTH), y_att.reshape(bp * sp, ATTN_WIDTH), wo),
                          final_g=row(norm_final) if last else None,
                          block_sums=(page_table, ck_t, l, half_s, bs - half_s))
        ksum = jnp.concatenate([ksum_a, ksum_b], axis=0)

        xs = _ffn(xs, row(norm_ffn1[l]), wg1, wu1, wd1, l, tm=n_tok_s)
        u, q, kt, vt = _proj(xs.reshape(1, n_tok_s, d), row(norm_mix[l]), wuq, wkv_t, l, tok_s, feat_s,
                             tm=n_tok_s, layer=0, n_layers=1)
        y_ssm, hr, hi = _ssm(u.reshape(bs, ss, SSM_WIDTH), state_ssm_re[l].reshape(bs, N_STATE),
                             state_ssm_im[l].reshape(bs, N_STATE), *ssm_consts, t_steps=ss)
        hr_s.append(hr)
        hi_s.append(hi)
        kn_t = kt.reshape(N_HEADS, HEAD_DIM, bs, ss).transpose(2, 0, 1, 3)
        vn_t = vt.reshape(N_HEADS, HEAD_DIM, bs, ss).transpose(2, 0, 1, 3)
        k_s.append(kn_t.transpose(0, 3, 1, 2))
        v_s.append(vn_t.transpose(0, 3, 1, 2))
        q4 = q.reshape(bs, ss, N_HEADS, HEAD_DIM).transpose(0, 2, 1, 3)
        sel = _select_sample(q4, ksum, page_table, past, n_blocks_past)
        att = _attn_sample(sel, page_table, q4, kn_t, vn_t, ck_t, cv_t, l, past)
        y_att = att.transpose(0, 2, 1, 3).reshape(n_tok_s, ATTN_WIDTH).astype(BF16)
        xs = _ffn(xs, row(norm_ffn2[l]), wg2, wu2, wd2, l, tm=n_tok_s,
                  premix=(y_ssm.reshape(n_tok_s, SSM_WIDTH), y_att, wo),
                  final_g=row(norm_final) if last else None)

    kt_all, vt_all = kv_p
    to_tokens = lambda t: t.reshape(depth, bp, N_HEADS, HEAD_DIM, sp).transpose(0, 1, 4, 2, 3)
    state = lambda hs, bn: jnp.stack(hs).reshape(depth, bn, SSM_GROUPS, SSM_STATE)
    return (xp.reshape(bp, sp, d), xs.reshape(bs, ss, d),
            to_tokens(kt_all), to_tokens(vt_all), state(hr_p, bp), state(hi_p, bp),
            jnp.stack(k_s), jnp.stack(v_s), state(hr_s, bs), state(hi_s, bs))
```

```python
import functools
import math

import numpy as np
import jax
import jax.numpy as jnp
from jax import lax
from jax.experimental import pallas as pl
from jax.experimental.pallas import tpu as pltpu

F32 = jnp.float32
BF16 = jnp.bfloat16

D_MODEL = 1024
SSM_WIDTH = 512
SSM_GROUP = 16
SSM_GROUPS = 32
SSM_STATE = 64
ATTN_WIDTH = 512
HEAD_DIM = 64
N_HEADS = 8
ROT_DIM = 16
ROPE_THETA = 500000.0
MOBA_BLOCK = 256
MOBA_TOPK = 3
PAGE_SIZE = 128
PAGES_PER_BLOCK = MOBA_BLOCK // PAGE_SIZE
D_FF = 2816
NORM_EPS = 1e-6

LANES = 128
SUBLANES = 8
MXU_DIM = 256
N_STATE = SSM_GROUPS * SSM_STATE
VMEM_LIMIT = 56 * 1024 * 1024
MASK_BIAS = -(2.0 ** 100)

NT_DIMS = (((1,), (1,)), ((), ()))


def _dot(a, b):
    return jnp.dot(a, b, preferred_element_type=F32)


def _dot_nt(a, b, precision=None):
    return lax.dot_general(a, b, NT_DIMS, preferred_element_type=F32, precision=precision)


def _rms(x, g):
    return x * lax.rsqrt(jnp.mean(x * x, axis=-1, keepdims=True) + NORM_EPS) * g


def _const_spec(shape):
    nd = len(shape)
    return pl.BlockSpec(shape, lambda *_: (0,) * nd, pipeline_mode=pl.Buffered(1))


def _layer_spec(stacked, layer):
    nd = stacked.ndim - 1
    return pl.BlockSpec((None,) + stacked.shape[1:], lambda *_: (layer,) + (0,) * nd,
                        pipeline_mode=pl.Buffered(1))


def _params(sem, vmem=VMEM_LIMIT):
    return pltpu.CompilerParams(dimension_semantics=sem, vmem_limit_bytes=vmem)


def _ff_chunks(d_ff):
    assert d_ff % MXU_DIM == 0
    n = d_ff // MXU_DIM
    sizes, left = [], n
    while left > 0:
        take = min(4, left)
        sizes.append(take * MXU_DIM)
        left -= take
    return sizes


def _block_sums_step(pt_ref, cache_hbm, ks_ref, pbuf, sem, *, layer, row0, n_pages, steps_per_row):
    pps = pbuf.shape[1]
    bps = pps // PAGES_PER_BLOCK
    step, n_steps = pl.program_id(0), pl.num_programs(0)
    slot = step % 2

    def copies(st, sl):
        base = (row0 + st // steps_per_row) * n_pages + (st % steps_per_row) * pps
        return [pltpu.make_async_copy(cache_hbm.at[layer, pt_ref[base + k]], pbuf.at[sl, k], sem.at[sl])
                for k in range(pps)]

    @pl.when(step == 0)
    def _():
        for c in copies(step, slot):
            c.start()

    @pl.when(step + 1 < n_steps)
    def _():
        for c in copies(step + 1, 1 - slot):
            c.start()

    for c in copies(step, slot):
        c.wait()

    part = step % steps_per_row

    @pl.when(part == 0)
    def _():
        ks_ref[...] = jnp.zeros(ks_ref.shape, F32)

    acc = ks_ref[...]
    lane = lax.broadcasted_iota(jnp.int32, acc.shape, 2)
    for j in range(bps):
        blk = pbuf[slot, PAGES_PER_BLOCK * j]
        for pp in range(1, PAGES_PER_BLOCK):
            blk = blk + pbuf[slot, PAGES_PER_BLOCK * j + pp]
        acc = jnp.where(lane == part * bps + j, jnp.sum(blk, axis=-1, keepdims=True), acc)
    ks_ref[...] = acc


def _ffn_body(*refs, premix, final, chunks, side):
    it = iter(refs)
    pt_ref = next(it) if side else None
    x_ref = next(it)
    if premix:
        ys_ref, ya_ref, wo_ref = next(it), next(it), next(it)
    g_ref, wg_ref, wu_ref, wd_ref = next(it), next(it), next(it), next(it)
    gf_ref = next(it) if final else None
    cache_hbm = next(it) if side else None
    o_ref = next(it)
    if side:
        ks_ref, pbuf, sem = next(it), next(it), next(it)
        _block_sums_step(pt_ref, cache_hbm, ks_ref, pbuf, sem, **side)

    x = x_ref[...]
    if premix:
        half = ys_ref.shape[1]
        x = x + _dot(ys_ref[...], wo_ref[:half, :]) + _dot(ya_ref[...], wo_ref[half:, :])
    xn = _rms(x, g_ref[...]).astype(BF16)
    acc = jnp.zeros(x.shape, F32)
    off = 0
    for sz in chunks:
        g = _dot(xn, wg_ref[:, off:off + sz])
        u = _dot(xn, wu_ref[:, off:off + sz])
        h = (jax.nn.silu(g) * u).astype(BF16)
        acc = acc + _dot(h, wd_ref[off:off + sz, :])
        off += sz
    y = x + 0.5 * acc
    if final:
        y = _rms(y, gf_ref[...])
    o_ref[...] = y


def _ffn(x, g, wg, wu, wd, layer, *, premix=None, final_g=None, tm, block_sums=None):
    m, d = x.shape
    d_ff = wg.shape[-1]
    assert m % tm == 0
    n_steps = m // tm
    row = lambda i, *_: (i, 0)
    args, specs = [x], [pl.BlockSpec((tm, d), row)]
    if premix is not None:
        ys, ya, wo = premix
        args += [ys, ya, wo]
        specs += [pl.BlockSpec((tm, ys.shape[1]), row), pl.BlockSpec((tm, ya.shape[1]), row),
                  _layer_spec(wo, layer)]
    args += [g, wg, wu, wd]
    specs += [_const_spec(g.shape), _layer_spec(wg, layer), _layer_spec(wu, layer), _layer_spec(wd, layer)]
    if final_g is not None:
        args.append(final_g)
        specs.append(_const_spec(final_g.shape))
    out_specs = pl.BlockSpec((tm, d), row)
    out_shape = jax.ShapeDtypeStruct((m, d), F32)
    side, scratch, prefetch = None, [], []
    if block_sums is not None:
        page_table, cache_t, layer, row0, n_rows = block_sums
        n_pages = page_table.shape[1]
        assert n_steps % n_rows == 0
        steps_per_row = n_steps // n_rows
        assert n_pages % (steps_per_row * PAGES_PER_BLOCK) == 0 and n_pages // PAGES_PER_BLOCK <= LANES
        pps = n_pages // steps_per_row
        page_shape = cache_t.shape[2:]
        side = dict(layer=layer, row0=row0, n_pages=n_pages, steps_per_row=steps_per_row)
        prefetch = [page_table.reshape(-1)]
        args.append(cache_t)
        specs.append(pl.BlockSpec(memory_space=pl.ANY))
        ks_block = page_shape[:2] + (LANES,)
        out_specs = [out_specs, pl.BlockSpec((None,) + ks_block, lambda i, *_: (i // steps_per_row, 0, 0, 0))]
        out_shape = [out_shape, jax.ShapeDtypeStruct((n_rows,) + ks_block, F32)]
        scratch = [pltpu.VMEM((2, pps) + page_shape, F32), pltpu.SemaphoreType.DMA((2,))]
    body = functools.partial(_ffn_body, premix=premix is not None, final=final_g is not None,
                             chunks=_ff_chunks(d_ff), side=side)
    return pl.pallas_call(
        body,
        grid_spec=pltpu.PrefetchScalarGridSpec(
            num_scalar_prefetch=len(prefetch), grid=(n_steps,), in_specs=specs, out_specs=out_specs,
            scratch_shapes=scratch),
        out_shape=out_shape,
        compiler_params=_params(("arbitrary",) if side else ("parallel",)), name="ffn")(*prefetch, *args)


def _rope_tables(positions):
    half = ROT_DIM // 2
    pos = np.asarray(positions, np.float64)
    inv = np.power(ROPE_THETA, -np.arange(half, dtype=np.float64) * 2.0 / ROT_DIM)
    ang = pos[:, None] * inv[None, :]
    cos, sin = np.cos(ang), np.sin(ang)
    d = np.arange(LANES) % HEAD_DIM
    f = d % half
    c = np.where(d[None, :] < ROT_DIM, cos[:, f], 1.0)
    s_next = np.where(d[None, :] < half, -sin[:, f], 0.0)
    s_prev = np.where((d[None, :] >= half) & (d[None, :] < ROT_DIM), sin[:, f], 0.0)
    tok = tuple(jnp.asarray(t, F32) for t in (c, s_next, s_prev))
    feat = (jnp.asarray(cos.T, F32), jnp.asarray(sin.T, F32))
    return tok, feat


def _proj_body(x_ref, g_ref, wuq_ref, wkv_ref, c_ref, sn_ref, sp_ref, ct_ref, st_ref, *rest, layer):
    u_ref, q_ref, kt_ref, vt_ref = rest[-4:]
    if len(kt_ref.shape) == 3:
        for l in range(kt_ref.shape[0]):
            if l != layer:
                kt_ref[l] = jnp.zeros(kt_ref.shape[1:], F32)
                vt_ref[l] = jnp.zeros(vt_ref.shape[1:], F32)
        kt_ref, vt_ref = kt_ref.at[layer], vt_ref.at[layer]
    half = ROT_DIM // 2
    hb = _rms(x_ref[...], g_ref[...]).astype(BF16)
    uq = _dot(hb, wuq_ref[...])
    u_ref[...] = uq[:, :SSM_WIDTH]
    c, sn, sp = c_ref[...], sn_ref[...], sp_ref[...]
    for j in range(ATTN_WIDTH // LANES):
        qc = uq[:, SSM_WIDTH + j * LANES:SSM_WIDTH + (j + 1) * LANES]
        q_ref[:, j * LANES:(j + 1) * LANES] = (
            qc * c + pltpu.roll(qc, LANES - half, axis=1) * sn + pltpu.roll(qc, half, axis=1) * sp)
    kv = _dot_nt(wkv_ref[...], hb)
    ct, st = ct_ref[...], st_ref[...]
    for h in range(N_HEADS):
        r = h * HEAD_DIM
        a, b = kv[r:r + half], kv[r + half:r + ROT_DIM]
        kt_ref[r:r + half, :] = a * ct - b * st
        kt_ref[r + half:r + ROT_DIM, :] = b * ct + a * st
        kt_ref[r + ROT_DIM:r + HEAD_DIM, :] = kv[r + ROT_DIM:r + HEAD_DIM]
    vt_ref[...] = kv[ATTN_WIDTH:]


def _proj(x3, g, wuq, wkv_t, w_layer, tok_tabs, feat_tabs, *, tm, layer, n_layers, prev=None):
    bn, sn, d = x3.shape
    assert sn % tm == 0
    tok_spec = pl.BlockSpec((tm, LANES), lambda b, s: (s, 0))
    feat_spec = pl.BlockSpec((ROT_DIM // 2, tm), lambda b, s: (0, s))
    in_specs = [pl.BlockSpec((None, tm, d), lambda b, s: (b, s, 0)), _const_spec(g.shape),
                _layer_spec(wuq, w_layer), _layer_spec(wkv_t, w_layer),
                tok_spec, tok_spec, tok_spec, feat_spec, feat_spec]
    args = [x3, g, wuq, wkv_t, *tok_tabs, *feat_tabs]
    aliases = {}
    if prev is not None:
        in_specs += [pl.BlockSpec(memory_space=pl.ANY)] * 2
        aliases = {len(args): 2, len(args) + 1: 3}
        args += list(prev)
    tok_out = pl.BlockSpec((None, tm, SSM_WIDTH), lambda b, s: (b, s, 0))
    if prev is None and n_layers > 1:
        feat_out = pl.BlockSpec((n_layers, None, ATTN_WIDTH, tm), lambda b, s: (0, b, 0, s))
    else:
        feat_out = pl.BlockSpec((None, None, ATTN_WIDTH, tm), lambda b, s: (layer, b, 0, s))
    stacked = jax.ShapeDtypeStruct((n_layers, bn, ATTN_WIDTH, sn), F32)
    return pl.pallas_call(
        functools.partial(_proj_body, layer=layer), grid=(bn, sn // tm), in_specs=in_specs,
        out_specs=[tok_out, tok_out, feat_out, feat_out],
        out_shape=[jax.ShapeDtypeStruct((bn, sn, SSM_WIDTH), F32),
                   jax.ShapeDtypeStruct((bn, sn, ATTN_WIDTH), F32), stacked, stacked],
        input_output_aliases=aliases, compiler_params=_params(("parallel", "parallel")),
        name="proj")(*args)


def _ssm_prep_body(ldt_ref, ar_ref, ai_ref, btr_ref, bti_ref, abr_ref, abi_ref, bbr_ref, bbi_ref):
    dt = jnp.exp(ldt_ref[...])
    ar, ai = ar_ref[...], ai_ref[...]
    mag = jnp.exp(ar * dt)
    abr = mag * jnp.cos(ai * dt)
    abi = mag * jnp.sin(ai * dt)
    abr_ref[...] = abr
    abi_ref[...] = abi
    den = ar * ar + ai * ai
    n_re, n_im = abr - 1.0, abi
    w_re = ((n_re * ar + n_im * ai) / den)[:, None, :]
    w_im = ((n_im * ar - n_re * ai) / den)[:, None, :]
    btr, bti = btr_ref[...], bti_ref[...]
    bbr_ref[...] = w_re * btr - w_im * bti
    bbi_ref[...] = w_re * bti + w_im * btr


def _ssm_prep(log_dt, a_re, a_im, b_re, b_im):
    g, p, h = b_re.shape
    shp = lambda *s: jax.ShapeDtypeStruct(s, F32)
    return pl.pallas_call(
        _ssm_prep_body, out_shape=[shp(g, p), shp(g, p), shp(g, h, p), shp(g, h, p)], name="ssm_prep",
    )(log_dt.reshape(g, 1), a_re, a_im, b_re.transpose(0, 2, 1), b_im.transpose(0, 2, 1))


def _ssm_layouts(abr, abi, bbr_t, bbi_t, c_re, c_im):
    g, h, p = bbr_t.shape
    n_tiles = g // 2
    blk = jnp.concatenate([bbr_t.reshape(n_tiles, 2, h, p), bbi_t.reshape(n_tiles, 2, h, p)], axis=1)
    grp_per_lane_tile = LANES // h
    place = np.zeros((n_tiles, 4, grp_per_lane_tile), np.float32)
    for j in range(n_tiles):
        for k in range(4):
            place[j, k, (2 * j + k % 2) % grp_per_lane_tile] = 1.0
    wbu = blk[:, :, None, :, :] * jnp.asarray(place)[:, :, :, None, None]
    wbu = wbu.transpose(0, 2, 3, 1, 4).reshape(n_tiles, LANES, 4 * p).astype(BF16)
    c_rows = jnp.concatenate([c_re.transpose(0, 2, 1).reshape(n_tiles, 2, p, h),
                              -c_im.transpose(0, 2, 1).reshape(n_tiles, 2, p, h)], axis=1)
    tiles_per_half = n_tiles // 2
    diag = np.zeros((tiles_per_half, 4, tiles_per_half, 2), np.float32)
    for t in range(tiles_per_half):
        for k in range(4):
            diag[t, k, t, k % 2] = 1.0
    cmat = (c_rows.reshape(2, tiles_per_half, 4, p, 1, 1, h)
            * jnp.asarray(diag)[None, :, :, None, :, :, None])
    cmat = cmat.reshape(2, tiles_per_half * 4 * p, tiles_per_half * 2 * h).astype(BF16)
    ar8 = jnp.broadcast_to(abr.reshape(1, g * p), (SUBLANES, g * p))
    ai8 = jnp.broadcast_to(abi.reshape(1, g * p), (SUBLANES, g * p))
    return wbu, cmat, ar8, ai8


def _perm_matrices(t):
    r = SUBLANES * t
    p = np.zeros((r, r), np.float32)
    for b in range(SUBLANES):
        for s in range(t):
            p[s * SUBLANES + b, b * t + s] = 1.0
    return jnp.asarray(p, BF16), jnp.asarray(p.T, BF16)


def _ssm_body(u_ref, h0r_ref, h0i_ref, p_ref, pt_ref, wbu_ref, cm_ref, ar_ref, ai_ref, d_ref, gw_ref,
              gb_ref, y_ref, hr_ref, hi_ref, bu_ref, *, t_steps, n_groups):
    r = SUBLANES * t_steps
    n_tiles = wbu_ref.shape[0]
    tile_w = 2 * LANES
    tiles_per_lane_tile = (LANES // SSM_GROUP) // 2
    rows = lambda g: slice(g * SUBLANES, (g + 1) * SUBLANES)

    @pl.when(pl.program_id(1) == 0)
    def _():
        hr_ref[...] = h0r_ref[...]
        hi_ref[...] = h0i_ref[...]

    perm = p_ref[...]
    u_tb = []
    for g in range(n_groups):
        u = u_ref[rows(g)].reshape(r, SSM_WIDTH)
        u_hi, u_lo = _split_bf16(u)
        uh = _dot(perm, u_hi)
        u_tb.append(uh + _dot(perm, u_lo))
        ub = uh.astype(BF16)
        for j in range(n_tiles):
            lt = j // tiles_per_lane_tile
            bu_ref[g, :, j * tile_w:(j + 1) * tile_w] = _dot(ub[:, lt * LANES:(lt + 1) * LANES], wbu_ref[j])

    for g in range(n_groups):
        hr = [hr_ref[rows(g), j * LANES:(j + 1) * LANES] for j in range(n_tiles)]
        hi = [hi_ref[rows(g), j * LANES:(j + 1) * LANES] for j in range(n_tiles)]
        for t in range(t_steps):
            tr = slice(t * SUBLANES, (t + 1) * SUBLANES)
            for j in range(n_tiles):
                cr = slice(j * tile_w, j * tile_w + LANES)
                ci = slice(j * tile_w + LANES, (j + 1) * tile_w)
                cs = slice(j * LANES, (j + 1) * LANES)
                ar, ai = ar_ref[:, cs], ai_ref[:, cs]
                nr = ar * hr[j] - ai * hi[j] + bu_ref[g, tr, cr]
                ni = ar * hi[j] + ai * hr[j] + bu_ref[g, tr, ci]
                bu_ref[g, tr, cr] = nr
                bu_ref[g, tr, ci] = ni
                hr[j], hi[j] = nr, ni
        hr_ref[rows(g), :] = jnp.concatenate(hr, axis=1)
        hi_ref[rows(g), :] = jnp.concatenate(hi, axis=1)

    half_cols = bu_ref.shape[2] // 2
    for g in range(n_groups):
        y = jnp.concatenate(
            [_dot(bu_ref[g, :, :half_cols].astype(BF16), cm_ref[0]),
             _dot(bu_ref[g, :, half_cols:].astype(BF16), cm_ref[1])], axis=1)
        y = y + d_ref[...] * u_tb[g]
        z = jax.nn.gelu(y)
        gate = jax.nn.sigmoid(_dot(z.astype(BF16), gw_ref[...]) + gb_ref[...])
        o = (z * gate).astype(BF16)
        y_ref[rows(g)] = _dot(pt_ref[...], o).astype(BF16).reshape((SUBLANES,) + y_ref.shape[1:])


SSM_GROUPS_PER_STEP = 2


def _ssm(u3, h0r, h0i, wbu, cmat, ar8, ai8, d, gw, gb, layer, *, t_steps):
    bn, ln, w = u3.shape
    n_groups = SSM_GROUPS_PER_STEP
    nb = n_groups * SUBLANES
    assert bn % nb == 0 and ln % t_steps == 0
    perm, perm_t = _perm_matrices(t_steps)
    r = SUBLANES * t_steps
    state_spec = pl.BlockSpec((nb, N_STATE), lambda b, c: (b, 0))
    seq_spec = pl.BlockSpec((nb, t_steps, w), lambda b, c: (b, c, 0))
    consts = [perm, perm_t, wbu, cmat, ar8, ai8, d, gw, gb]
    return pl.pallas_call(
        functools.partial(_ssm_body, t_steps=t_steps, n_groups=n_groups),
        grid=(bn // nb, ln // t_steps),
        in_specs=[seq_spec, state_spec, state_spec]
        + [_layer_spec(c, layer) if c is gw else _const_spec(c.shape) for c in consts],
        out_specs=[seq_spec, state_spec, state_spec],
        out_shape=[jax.ShapeDtypeStruct((bn, ln, w), BF16),
                   jax.ShapeDtypeStruct((bn, N_STATE), F32), jax.ShapeDtypeStruct((bn, N_STATE), F32)],
        scratch_shapes=[pltpu.VMEM((n_groups, r, 2 * N_STATE), F32)],
        compiler_params=_params(("parallel", "arbitrary")), name="ssm",
    )(u3, h0r, h0i, *consts)


def _select_rows(g, valid, n_idx, k):
    g = jnp.where(valid, g, -jnp.inf)
    cnt = jnp.zeros(g.shape, jnp.int32)
    for m in range(g.shape[0]):
        row = g[m:m + 1, :]
        beats = (row > g) | ((row == g) & (m < n_idx))
        cnt = cnt + beats.astype(jnp.int32)
    return (cnt < k) & valid


def _split_bf16(x):
    hi = x.astype(BF16)
    return hi, (x - hi.astype(F32)).astype(BF16)


def _attn_body(q_ref, kt_ref, vt_ref, o_ref, kaug_ref, vtb_ref, s_ref, *, n_blocks):
    pair_w = 2 * HEAD_DIM
    blk = MOBA_BLOCK
    lane = lax.broadcasted_iota(jnp.int32, (blk, pair_w), 1)

    sums = [jnp.sum(kt_ref[:, n * blk:(n + 1) * blk], axis=1, keepdims=True) for n in range(n_blocks)]
    km_t = jnp.concatenate(sums + [jnp.zeros((pair_w, pair_w - n_blocks), F32)], axis=1)
    km = (km_t * (1.0 / blk)).T[:SUBLANES]
    l8 = lax.broadcasted_iota(jnp.int32, km.shape, 1)
    km_split = [_split_bf16(jnp.where(l8 < HEAD_DIM, km, 0.0)), _split_bf16(jnp.where(l8 >= HEAD_DIM, km, 0.0))]
    rows = lax.broadcasted_iota(jnp.int32, (HEAD_DIM, blk), 0)
    for n in range(n_blocks):
        kb = kt_ref[:, n * blk:(n + 1) * blk].astype(BF16)
        ind = (rows == n).astype(BF16)
        kaug_ref[0, n] = jnp.concatenate([kb[:HEAD_DIM], ind], axis=0)
        kaug_ref[1, n] = jnp.concatenate([ind, kb[HEAD_DIM:]], axis=0)
        vtb_ref[n] = vt_ref[:, n * blk:(n + 1) * blk].astype(BF16)

    n_idx = lax.broadcasted_iota(jnp.int32, (SUBLANES, blk), 0)
    eye = (lax.broadcasted_iota(jnp.int32, (blk, blk), 0)
           == lax.broadcasted_iota(jnp.int32, (blk, blk), 1)).astype(BF16)
    causal = (lax.broadcasted_iota(jnp.int32, (blk, blk), 1)
              <= lax.broadcasted_iota(jnp.int32, (blk, blk), 0))
    pad = jnp.zeros((HEAD_DIM - SUBLANES, blk), F32)
    scale = HEAD_DIM ** -0.5

    for t in range(n_blocks):
        q = q_ref[t * blk:(t + 1) * blk, :]
        qs = q * scale
        if t > MOBA_TOPK:
            q_hi, q_lo = _split_bf16(q)
            bias_rows = []
            for h in range(2):
                k_hi, k_lo = km_split[h]
                g_t = _dot_nt(k_hi, q_hi) + _dot_nt(k_hi, q_lo) + _dot_nt(k_lo, q_hi)
                sel = _select_rows(g_t, n_idx < t, n_idx, MOBA_TOPK) | (n_idx == t)
                bias_rows.append(jnp.where(sel, 0.0, MASK_BIAS))
            b_mat = jnp.concatenate([bias_rows[1], pad, bias_rows[0], pad], axis=0).astype(BF16)
            bias_q = _dot_nt(eye, b_mat)
        else:
            bias_q = jnp.zeros((blk, pair_w), F32)
        outs = []
        for h in range(2):
            mine = (lane < HEAD_DIM) if h == 0 else (lane >= HEAD_DIM)
            qa = jnp.where(mine, qs, bias_q).astype(BF16)
            m_run = None
            for n in range(t + 1):
                s = _dot(qa, kaug_ref[h, n])
                if n == t:
                    s = jnp.where(causal, s, -jnp.inf)
                s_ref[h, n] = s
                mx = jnp.maximum(s[:, :LANES], s[:, LANES:])
                m_run = mx if m_run is None else jnp.maximum(m_run, mx)
            m = jnp.broadcast_to(jnp.max(m_run, axis=1, keepdims=True), (blk, LANES))
            l_run = jnp.zeros((blk, LANES), F32)
            acc = jnp.zeros((blk, pair_w), F32)
            for n in range(t + 1):
                p0 = jnp.exp(s_ref[h, n, :, :LANES] - m)
                p1 = jnp.exp(s_ref[h, n, :, LANES:] - m)
                l_run = l_run + (p0 + p1)
                acc = acc + _dot_nt(jnp.concatenate([p0, p1], axis=1).astype(BF16), vtb_ref[n])
            outs.append(acc / jnp.sum(l_run, axis=1, keepdims=True))
        o_ref[t * blk:(t + 1) * blk, :] = jnp.where(lane < HEAD_DIM, outs[0], outs[1]).astype(o_ref.dtype)


def _attn_prompt(q3, kt_all, vt_all, layer):
    bn, sn, w = q3.shape
    pair_w = 2 * HEAD_DIM
    n_blocks = sn // MOBA_BLOCK
    assert sn % MOBA_BLOCK == 0 and MOBA_TOPK <= n_blocks <= SUBLANES
    kv_spec = pl.BlockSpec((None, None, pair_w, sn), lambda b, p: (layer, b, p, 0))
    q_spec = pl.BlockSpec((None, sn, pair_w), lambda b, p: (b, 0, p))
    return pl.pallas_call(
        functools.partial(_attn_body, n_blocks=n_blocks),
        grid=(bn, w // pair_w),
        in_specs=[q_spec, kv_spec, kv_spec], out_specs=q_spec,
        out_shape=jax.ShapeDtypeStruct((bn, sn, w), BF16),
        scratch_shapes=[pltpu.VMEM((2, n_blocks, pair_w, MOBA_BLOCK), BF16),
                        pltpu.VMEM((n_blocks, pair_w, MOBA_BLOCK), BF16),
                        pltpu.VMEM((2, n_blocks, MOBA_BLOCK, MOBA_BLOCK), F32)],
        compiler_params=_params(("parallel", "parallel")), name="attn_prompt",
    )(q3, kt_all, vt_all)


def _select_body(q_ref, ks_ref, pt_ref, o_ref, *, past, n_blocks, n_pages):
    n_new = q_ref.shape[1]
    lane = lax.broadcasted_iota(jnp.int32, (n_new, LANES), 1)
    q_pos = past + lax.broadcasted_iota(jnp.int32, (n_new, LANES), 0)
    valid = (lane < q_pos // MOBA_BLOCK) & (lane < n_blocks)
    pt_row = jnp.broadcast_to(pt_ref[...], (n_new, n_pages))
    page_lane = lax.broadcasted_iota(jnp.int32, (n_new, n_pages), 1)
    for h in range(N_HEADS):
        km = ks_ref[h] * (1.0 / MOBA_BLOCK)
        g = jnp.dot(q_ref[h], km, preferred_element_type=F32, precision=lax.Precision.HIGHEST)
        g = jnp.where(valid, g, -jnp.inf)
        out = jnp.zeros((n_new, LANES), jnp.int32)
        for j in range(MOBA_TOPK):
            best = jnp.max(g, axis=1, keepdims=True)
            idx = jnp.min(jnp.where(g == best, lane, LANES), axis=1, keepdims=True)
            ok = jnp.sum(jnp.where((lane == idx) & valid, 1, 0), axis=1, keepdims=True)
            out = jnp.where(lane == MOBA_TOPK * PAGES_PER_BLOCK + j, ok, out)
            for pp in range(PAGES_PER_BLOCK):
                page = jnp.minimum(jnp.minimum(idx, n_blocks - 1) * PAGES_PER_BLOCK + pp, n_pages - 1)
                pool = jnp.sum(jnp.where(page_lane == page, pt_row, 0), axis=1, keepdims=True)
                out = jnp.where(lane == j * PAGES_PER_BLOCK + pp, pool, out)
            g = jnp.where(lane == idx, -jnp.inf, g)
        o_ref[h] = out


def _select_sample(q4, ksum, page_table, past, n_blocks):
    bn, hn, qn, dh = q4.shape
    n_pages = page_table.shape[1]
    return pl.pallas_call(
        functools.partial(_select_body, past=past, n_blocks=n_blocks, n_pages=n_pages),
        grid=(bn,),
        in_specs=[pl.BlockSpec((None, hn, qn, dh), lambda b: (b, 0, 0, 0)),
                  pl.BlockSpec((None, hn, dh, LANES), lambda b: (b, 0, 0, 0)),
                  pl.BlockSpec((None, 1, n_pages), lambda b: (b, 0, 0))],
        out_specs=pl.BlockSpec((None, hn, qn, LANES), lambda b: (b, 0, 0, 0)),
        out_shape=jax.ShapeDtypeStruct((bn, hn, qn, LANES), jnp.int32),
        compiler_params=_params(("parallel",)), name="select_sample",
    )(q4, ksum, page_table.reshape(bn, 1, n_pages))


def _attn_sample_body(pages_ref, ok_ref, q_ref, kn_ref, vn_ref, ck_hbm, cv_hbm, o_ref, kbuf, vbuf, sem,
                      *, past, n_new, layer):
    per_q = MOBA_TOPK * PAGES_PER_BLOCK
    n_sel = n_new * per_q
    n_fetch = n_sel + PAGES_PER_BLOCK
    n_heads = pl.num_programs(1)
    step = pl.program_id(0) * n_heads + pl.program_id(1)
    n_steps = pl.num_programs(0) * n_heads
    slot = step % 2

    def copies(st, sl, i):
        page = pages_ref[st * n_fetch + i]
        head = st % n_heads
        return (pltpu.make_async_copy(ck_hbm.at[layer, page, head], kbuf.at[sl, i], sem.at[0, sl]),
                pltpu.make_async_copy(cv_hbm.at[layer, page, head], vbuf.at[sl, i], sem.at[1, sl]))

    def start_all(st, sl):
        for i in range(n_fetch):
            for c in copies(st, sl, i):
                c.start(priority=i % 2)

    @pl.when(step == 0)
    def _():
        start_all(step, slot)

    @pl.when(step + 1 < n_steps)
    def _():
        start_all(step + 1, 1 - slot)

    for i in range(n_fetch):
        for c in copies(step, slot, i):
            c.wait()

    ok_base = step * n_new * MOBA_TOPK
    scale = HEAD_DIM ** -0.5
    blk_w = PAGES_PER_BLOCK * PAGE_SIZE
    blk_lane = lax.broadcasted_iota(jnp.int32, (n_new, blk_w), 1)
    new_lane = lax.broadcasted_iota(jnp.int32, (n_new, PAGE_SIZE), 1)

    def block(buf, first):
        return jnp.concatenate([buf[slot, first + pp] for pp in range(PAGES_PER_BLOCK)], axis=1).astype(BF16)

    qb = (q_ref[...] * scale).astype(BF16)
    blk_row = lax.broadcasted_iota(jnp.int32, (n_new, blk_w), 0)
    out_row = lax.broadcasted_iota(jnp.int32, (n_new, HEAD_DIM), 0)
    q_pos = past + blk_row
    own_start = (q_pos // MOBA_BLOCK) * MOBA_BLOCK

    scores = []
    for j in range(MOBA_TOPK):
        s_j = jnp.full((n_new, blk_w), -jnp.inf, F32)
        for qi in range(n_new):
            ok = ok_ref[ok_base + qi * MOBA_TOPK + j] > 0
            s = _dot(qb, block(kbuf, qi * per_q + j * PAGES_PER_BLOCK))
            s_j = jnp.where((blk_row == qi) & ok, s, s_j)
        scores.append(s_j)
    pos = past - blk_w + blk_lane
    scores.append(jnp.where((pos >= own_start) & (pos <= q_pos), _dot(qb, block(kbuf, n_sel)), -jnp.inf))
    new_pos = past + new_lane
    s_new = jnp.where((new_lane < n_new) & (new_pos >= own_start[:, :PAGE_SIZE])
                      & (new_pos <= q_pos[:, :PAGE_SIZE]),
                      _dot(qb, kn_ref[...].astype(BF16)), -jnp.inf)

    m_run = scores[0]
    for s in scores[1:]:
        m_run = jnp.maximum(m_run, s)
    m = jnp.maximum(jnp.max(m_run, axis=1, keepdims=True), jnp.max(s_new, axis=1, keepdims=True))
    p_new = jnp.exp(s_new - m)
    probs = [jnp.exp(s - m) for s in scores]
    l_run = probs[0]
    for p in probs[1:]:
        l_run = l_run + p
    l = jnp.sum(l_run, axis=1, keepdims=True) + jnp.sum(p_new, axis=1, keepdims=True)
    acc = (_dot_nt(p_new.astype(BF16), vn_ref[...].astype(BF16))
           + _dot_nt(probs[MOBA_TOPK].astype(BF16), block(vbuf, n_sel)))
    for j in range(MOBA_TOPK):
        pb = probs[j].astype(BF16)
        for qi in range(n_new):
            o = _dot_nt(pb, block(vbuf, qi * per_q + j * PAGES_PER_BLOCK))
            acc = acc + jnp.where(out_row == qi, o, 0.0)
    o_ref[...] = acc / l


def _attn_sample(sel, page_table, q4, kn_t, vn_t, ck_t, cv_t, layer, past):
    bn, hn, qn, dh = q4.shape
    n_pages = page_table.shape[1]
    assert qn == SUBLANES
    pad = ((0, 0), (0, 0), (0, 0), (0, PAGE_SIZE - qn))
    kn_t, vn_t = jnp.pad(kn_t, pad), jnp.pad(vn_t, pad)
    per_q = MOBA_TOPK * PAGES_PER_BLOCK
    n_fetch = qn * per_q + PAGES_PER_BLOCK
    tail_pages = jnp.broadcast_to(page_table[:, None, n_pages - PAGES_PER_BLOCK:], (bn, hn, PAGES_PER_BLOCK))
    pages = jnp.concatenate([sel[..., :per_q].reshape(bn, hn, qn * per_q), tail_pages], axis=-1)
    ok = sel[..., per_q:per_q + MOBA_TOPK]

    bh = lambda shape: pl.BlockSpec((None, None) + shape, lambda b, h, pages_r, ok_r: (b, h, 0, 0))
    any_spec = pl.BlockSpec(memory_space=pl.ANY)
    return pl.pallas_call(
        functools.partial(_attn_sample_body, past=past, n_new=qn, layer=layer),
        grid_spec=pltpu.PrefetchScalarGridSpec(
            num_scalar_prefetch=2, grid=(bn, hn),
            in_specs=[bh((qn, dh)), bh((dh, PAGE_SIZE)), bh((dh, PAGE_SIZE)), any_spec, any_spec],
            out_specs=bh((qn, dh)),
            scratch_shapes=[pltpu.VMEM((2, n_fetch, dh, PAGE_SIZE), F32),
                            pltpu.VMEM((2, n_fetch, dh, PAGE_SIZE), F32),
                            pltpu.SemaphoreType.DMA((2, 2))]),
        out_shape=jax.ShapeDtypeStruct((bn, hn, qn, dh), F32),
        compiler_params=_params(("arbitrary", "arbitrary")), name="attn_sample",
    )(pages.reshape(-1), ok.reshape(-1), q4, kn_t, vn_t, ck_t, cv_t)


FFN_TM = 512
PROJ_TM = 512
SSM_T_PROMPT = 64


def kernel(x_prompt, x_sample, cache_k, cache_v, state_ssm_re, state_ssm_im, page_table, norm_ffn1, ffn1_w_gate, ffn1_w_up, ffn1_w_down, norm_mix, w_in, ssm_log_dt, ssm_a_re, ssm_a_im, ssm_b_re, ssm_b_im, ssm_c_re, ssm_c_im, ssm_d, glu_w, glu_b, w_out, norm_ffn2, ffn2_w_gate, ffn2_w_up, ffn2_w_down, norm_final):
    depth = norm_ffn1.shape[0]
    bp, sp, d = x_prompt.shape
    bs, ss, _ = x_sample.shape
    n_pages = page_table.shape[1]
    past = n_pages * PAGE_SIZE
    n_blocks_past = past // MOBA_BLOCK
    assert n_blocks_past >= MOBA_TOPK

    ck_t = cache_k.transpose(0, 1, 3, 4, 2)
    cv_t = cache_v.transpose(0, 1, 3, 4, 2)

    tok_p, feat_p = _rope_tables(np.arange(sp))
    pos_s = np.tile(past + np.arange(ss), bs)
    tok_s, feat_s = _rope_tables(pos_s)

    xp = x_prompt.reshape(bp * sp, d)
    xs = x_sample.reshape(bs * ss, d)
    zeros_p = jnp.zeros((bp, N_STATE), F32)
    row = lambda v: v.reshape(1, -1)

    wg1, wu1, wd1 = ffn1_w_gate.astype(BF16), ffn1_w_up.astype(BF16), ffn1_w_down.astype(BF16)
    wg2, wu2, wd2 = ffn2_w_gate.astype(BF16), ffn2_w_up.astype(BF16), ffn2_w_down.astype(BF16)
    wuq = w_in[:, :, :SSM_WIDTH + ATTN_WIDTH].astype(BF16)
    wkv_t = w_in[:, :, SSM_WIDTH + ATTN_WIDTH:].transpose(0, 2, 1).astype(BF16)
    wo = w_out.astype(BF16)
    gw = glu_w.astype(BF16)

    kv_p = None
    hr_p, hi_p, k_s, v_s, hr_s, hi_s = [], [], [], [], [], []
    n_tok_s = bs * ss
    for l in range(depth):
        abr, abi, bbr_t, bbi_t = _ssm_prep(ssm_log_dt[l], ssm_a_re[l], ssm_a_im[l], ssm_b_re[l], ssm_b_im[l])
        wbu, cmat, ar8, ai8 = _ssm_layouts(abr, abi, bbr_t, bbi_t, ssm_c_re[l], ssm_c_im[l])
        ssm_consts = (wbu, cmat, ar8, ai8, row(ssm_d[l]), gw, row(glu_b[l]), l)
        last = l == depth - 1

        half_s = bs // 2
        xp, ksum_a = _ffn(xp, row(norm_ffn1[l]), wg1, wu1, wd1, l, tm=FFN_TM,
                          block_sums=(page_table, ck_t, l, 0, half_s))
        u, q, kt_all, vt_all = _proj(xp.reshape(bp, sp, d), row(norm_mix[l]), wuq, wkv_t, l, tok_p, feat_p,
                                     tm=PROJ_TM, layer=l, n_layers=depth, prev=kv_p)
        kv_p = (kt_all, vt_all)
        y_ssm, hr, hi = _ssm(u, zeros_p, zeros_p, *ssm_consts, t_steps=SSM_T_PROMPT)
        hr_p.append(hr)
        hi_p.append(hi)
        y_att = _attn_prompt(q, kt_all, vt_all, l)
        xp, ksum_b = _ffn(xp, row(norm_ffn2[l]), wg2, wu2, wd2, l, tm=FFN_TM,
                          premix=(y_ssm.reshape(bp * sp, SSM_WIDTH), y_att.reshape(bp * sp, ATTN_WIDTH), wo),
                          final_g=row(norm_final) if last else None,
                          block_sums=(page_table, ck_t, l, half_s, bs - half_s))
        ksum = jnp.concatenate([ksum_a, ksum_b], axis=0)

        xs = _ffn(xs, row(norm_ffn1[l]), wg1, wu1, wd1, l, tm=n_tok_s)
        u, q, kt, vt = _proj(xs.reshape(1, n_tok_s, d), row(norm_mix[l]), wuq, wkv_t, l, tok_s, feat_s,
                             tm=n_tok_s, layer=0, n_layers=1)
        y_ssm, hr, hi = _ssm(u.reshape(bs, ss, SSM_WIDTH), state_ssm_re[l].reshape(bs, N_STATE),
                             state_ssm_im[l].reshape(bs, N_STATE), *ssm_consts, t_steps=ss)
        hr_s.append(hr)
        hi_s.append(hi)
        kn_t = kt.reshape(N_HEADS, HEAD_DIM, bs, ss).transpose(2, 0, 1, 3)
        vn_t = vt.reshape(N_HEADS, HEAD_DIM, bs, ss).transpose(2, 0, 1, 3)
        k_s.append(kn_t.transpose(0, 3, 1, 2))
        v_s.append(vn_t.transpose(0, 3, 1, 2))
        q4 = q.reshape(bs, ss, N_HEADS, HEAD_DIM).transpose(0, 2, 1, 3)
        sel = _select_sample(q4, ksum, page_table, past, n_blocks_past)
        att = _attn_sample(sel, page_table, q4, kn_t, vn_t, ck_t, cv_t, l, past)
        y_att = att.transpose(0, 2, 1, 3).reshape(n_tok_s, ATTN_WIDTH).astype(BF16)
        xs = _ffn(xs, row(norm_ffn2[l]), wg2, wu2, wd2, l, tm=n_tok_s,
                  premix=(y_ssm.reshape(n_tok_s, SSM_WIDTH), y_att, wo),
                  final_g=row(norm_final) if last else None)

    kt_all, vt_all = kv_p
    to_tokens = lambda t: t.reshape(depth, bp, N_HEADS, HEAD_DIM, sp).transpose(0, 1, 4, 2, 3)
    state = lambda hs, bn: jnp.stack(hs).reshape(depth, bn, SSM_GROUPS, SSM_STATE)
    return (xp.reshape(bp, sp, d), xs.reshape(bs, ss, d),
            to_tokens(kt_all), to_tokens(vt_all), state(hr_p, bp), state(hi_p, bp),
            jnp.stack(k_s), jnp.stack(v_s), state(hr_s, bs), state(hi_s, bs))
```
